```python
import math
import jax, jax.numpy as jnp
from jax import lax
import numpy as np

D_MODEL = 1024
BATCH = 16
SEQ = 2048
DEPTH = 4

N_MIXERS = 2
N_HEADS = 16
HEAD_DIM = D_MODEL // N_HEADS
D_ATTN = N_HEADS * HEAD_DIM
DILATED_GROUPS = ((128, 1), (512, 4), (2048, 16))
N_GROUPS = len(DILATED_GROUPS)
BLOCK = 128
D_FF = 7 * D_MODEL // 2
N_EXPERTS = 8
TOP_K = 2
RMS_EPS = 1e-6

kernel_name = 'hybrid_dilated_stickbreaking_moe'


def rmsnorm(x, g):
    xf = x.astype(jnp.float32)
    y = xf * lax.rsqrt(jnp.mean(xf * xf, axis=-1, keepdims=True) + RMS_EPS)
    return (y * g.astype(jnp.float32)).astype(x.dtype)


def alibi_slopes(n):
    return jnp.asarray(2.0 ** (-8.0 * np.arange(1, n + 1) / n), dtype=jnp.float32)


def dilated_window_branch(q, k, v, window, dilation, slopes):
    b, s, h, dh = q.shape
    span = window // dilation
    n_prev = -(-span // BLOCK)
    sub_len = s // dilation
    nb = -(-sub_len // BLOCK)
    sub_pad = nb * BLOCK

    def to_sub(t):
        t = t.reshape(b, sub_len, dilation, h, dh).transpose(0, 2, 1, 3, 4)
        t = jnp.pad(t, ((0, 0), (0, 0), (0, sub_pad - sub_len), (0, 0), (0, 0)))
        return t.reshape(b, dilation, nb, BLOCK, h, dh)

    def band(t):
        tp = jnp.pad(t, ((0, 0), (0, 0), (n_prev, 0), (0, 0), (0, 0), (0, 0)))
        return jnp.concatenate([tp[:, :, i:i + nb] for i in range(n_prev + 1)], axis=3)

    qb = to_sub(q)
    kband = band(to_sub(k))
    vband = band(to_sub(v)).astype(jnp.float32)
    nk = (n_prev + 1) * BLOCK

    a = jnp.arange(BLOCK)[:, None]
    m = jnp.arange(nk)[None, :]
    rel = n_prev * BLOCK + a - m
    kpos = (jnp.arange(nb) * BLOCK)[:, None, None] + (a - rel)[None]
    valid = ((rel >= 0) & (rel <= span))[None] & (kpos >= 0)
    bias = -slopes[:, None, None] * (rel * dilation).astype(jnp.float32)[None]

    scores = jnp.einsum('brnqhd,brnkhd->brnhqk', qb, kband,
                        preferred_element_type=jnp.float32) * (1.0 / math.sqrt(dh))
    scores = jnp.where(valid[None, None, :, None], scores + bias, -jnp.inf)
    s_max = jnp.max(scores, axis=-1, keepdims=True)
    p = jnp.exp(scores - s_max)
    z = jnp.sum(p, axis=-1)
    out = jnp.einsum('brnhqk,brnkhd->brnqhd', p, vband)
    z_q = z.transpose(0, 1, 2, 4, 3)
    out = out / z_q[..., None]
    lse = s_max[..., 0].transpose(0, 1, 2, 4, 3) + jnp.log(z_q)

    out = out.reshape(b, dilation, sub_pad, h, dh)[:, :, :sub_len]
    out = out.transpose(0, 2, 1, 3, 4).reshape(b, s, h, dh)
    lse = lse.reshape(b, dilation, sub_pad, h)[:, :, :sub_len]
    lse = lse.transpose(0, 2, 1, 3).reshape(b, s, h)
    return out, lse


def dilated_mixer(hn, w_in, w_out, slopes):
    b, s, _ = hn.shape
    proj = (hn @ w_in).reshape(b, s, N_GROUPS, 3, N_HEADS, HEAD_DIM)
    outs, lses = [], []
    for g, (window, dilation) in enumerate(DILATED_GROUPS):
        o, l = dilated_window_branch(proj[:, :, g, 0], proj[:, :, g, 1], proj[:, :, g, 2],
                                     window, dilation, slopes)
        outs.append(o)
        lses.append(l)
    wts = jax.nn.softmax(jnp.stack(lses, axis=0), axis=0)
    y = jnp.sum(wts[..., None] * jnp.stack(outs, axis=0), axis=0)
    return y.reshape(b, s, D_ATTN).astype(hn.dtype) @ w_out


def stick_breaking_mixer(hn, w_qkv, w_out):
    b, s, _ = hn.shape
    qkv = (hn @ w_qkv).reshape(b, s, 3, N_HEADS, HEAD_DIM)
    q, k, v = qkv[:, :, 0], qkv[:, :, 1], qkv[:, :, 2]
    scale = 1.0 / math.sqrt(HEAD_DIM)
    outs = []
    for blk in range(s // BLOCK):
        start = blk * BLOCK
        end = start + BLOCK
        z = jnp.einsum('bqhd,bkhd->bhqk', q[:, start:end], k[:, :end],
                       preferred_element_type=jnp.float32) * scale
        t_idx = start + jnp.arange(BLOCK)[:, None]
        s_idx = jnp.arange(end)[None, :]
        strict = s_idx < t_idx
        log_beta = jax.nn.log_sigmoid(z)
        log_1mb = jnp.where(strict, jax.nn.log_sigmoid(-z), 0.0)
        after = lax.cumsum(log_1mb, axis=3, reverse=True) - log_1mb
        att = jnp.where(strict, jnp.exp(log_beta + after), 0.0)
        outs.append(jnp.einsum('bhqk,bkhd->bqhd', att, v[:, :end].astype(jnp.float32)))
    y = jnp.concatenate(outs, axis=1)
    return y.reshape(b, s, D_ATTN).astype(hn.dtype) @ w_out


def swiglu(hn, w_gate_up, w_down):
    g, u = jnp.split(hn @ w_gate_up, 2, axis=-1)
    return (jax.nn.silu(g) * u) @ w_down


def moe_swiglu(hn, w_router, w_gate_up, w_down):
    logits = jnp.einsum('bsd,de->bse', hn, w_router, preferred_element_type=jnp.float32)
    top_val, top_idx = lax.top_k(logits, TOP_K)
    top_w = jax.nn.softmax(top_val, axis=-1)
    gates = jnp.sum(jax.nn.one_hot(top_idx, N_EXPERTS, dtype=jnp.float32) * top_w[..., None], axis=-2)
    y = jnp.zeros_like(hn)
    for e in range(N_EXPERTS):
        y = y + gates[..., e:e + 1].astype(hn.dtype) * swiglu(hn, w_gate_up[e], w_down[e])
    return y


def setup_inputs(seed: int = 0) -> dict:
    key = jax.random.key(seed)
    ks = jax.random.split(key, 14)
    n_a = (DEPTH + 1) // 2
    n_b = DEPTH // 2

    def w(k, shape, fan_in):
        return jax.random.normal(k, shape, jnp.float32) * (fan_in ** -0.5)

    return {
        'x': jax.random.normal(ks[0], (BATCH, SEQ, D_MODEL), jnp.float32),
        'mix_norm': 1.0 + 0.02 * jax.random.normal(ks[1], (DEPTH, D_MODEL), jnp.float32),
        'ffn_norm': 1.0 + 0.02 * jax.random.normal(ks[2], (DEPTH, D_MODEL), jnp.float32),
        'dil_w_in': w(ks[3], (n_a, D_MODEL, N_GROUPS * 3 * D_ATTN), D_MODEL),
        'dil_w_out': w(ks[4], (n_a, D_ATTN, D_MODEL), D_ATTN),
        'sb_w_qkv': w(ks[5], (n_b, D_MODEL, 3 * D_ATTN), D_MODEL),
        'sb_w_out': w(ks[6], (n_b, D_ATTN, D_MODEL), D_ATTN),
        'ffn_w_gate_up': w(ks[7], (n_a, D_MODEL, 2 * D_FF), D_MODEL),
        'ffn_w_down': w(ks[8], (n_a, D_FF, D_MODEL), D_FF),
        'moe_w_router': w(ks[9], (n_b, D_MODEL, N_EXPERTS), D_MODEL),
        'moe_w_gate_up': w(ks[10], (n_b, N_EXPERTS, D_MODEL, 2 * D_FF), D_MODEL),
        'moe_w_down': w(ks[11], (n_b, N_EXPERTS, D_FF, D_MODEL), D_FF),
        'final_norm': 1.0 + 0.02 * jax.random.normal(ks[12], (D_MODEL,), jnp.float32),
    }


def reference(x, mix_norm, ffn_norm, dil_w_in, dil_w_out, sb_w_qkv, sb_w_out,
              ffn_w_gate_up, ffn_w_down, moe_w_router, moe_w_gate_up, moe_w_down, final_norm):
    slopes = alibi_slopes(N_HEADS)
    h = x
    for i in range(DEPTH):
        j = i // 2
        hn = rmsnorm(h, mix_norm[i])
        if i % N_MIXERS == 0:
            h = h + dilated_mixer(hn, dil_w_in[j], dil_w_out[j], slopes)
        else:
            h = h + stick_breaking_mixer(hn, sb_w_qkv[j], sb_w_out[j])
        hn = rmsnorm(h, ffn_norm[i])
        if i % 2 == 0:
            h = h + swiglu(hn, ffn_w_gate_up[j], ffn_w_down[j])
        else:
            h = h + moe_swiglu(hn, moe_w_router[j], moe_w_gate_up[j], moe_w_down[j])
    return rmsnorm(h, final_norm)
```

```python
import functools

import numpy as np
import jax
import jax.numpy as jnp
from jax import lax
from jax.experimental import pallas as pl
from jax.experimental.pallas import tpu as pltpu

D_MODEL = 1024
N_HEADS = 16
HEAD_DIM = 64
D_ATTN = N_HEADS * HEAD_DIM
DILATED_GROUPS = ((128, 1), (512, 4), (2048, 16))
N_GROUPS = len(DILATED_GROUPS)
BLOCK = 128
D_FF = 3584
N_EXPERTS = 8
TOP_K = 2
RMS_EPS = 1e-6
ATTN_SCALE = 0.125

LANES = 128
VMEM_LIMIT_BYTES = 56 * 1024 * 1024

TM = 512
TN = 1024
TN_FF = 512
TM_GROUP = 512
TM_COMBINE = 256
TM_EPILOGUE = 256

F32 = jnp.float32
BF16 = jnp.bfloat16

ALIBI_SLOPES = tuple(float(np.float32(2.0 ** (-8.0 * (h + 1) / N_HEADS))) for h in range(N_HEADS))


def _params(n_axes):
    return pltpu.CompilerParams(dimension_semantics=("arbitrary",) * n_axes, vmem_limit_bytes=VMEM_LIMIT_BYTES)


def _dot(a, b):
    return jnp.dot(a, b, preferred_element_type=F32)


def _dot_nt(a, b):
    return lax.dot_general(a, b, (((1,), (1,)), ((), ())), preferred_element_type=F32)


def _split_bf16(x):
    hi = x.astype(BF16)
    lo = (x - hi.astype(F32)).astype(BF16)
    return hi, lo


def _rmsnorm_rows(x, g):
    return x * lax.rsqrt(jnp.mean(x * x, axis=-1, keepdims=True) + RMS_EPS) * g


def _rmsnorm_body(x_ref, g_ref, o_ref):
    o_ref[...] = _rmsnorm_rows(x_ref[...], g_ref[...]).astype(o_ref.dtype)


def _rmsnorm(x, g, out_dtype):
    t, d = x.shape
    return pl.pallas_call(
        _rmsnorm_body,
        grid=(t // TM,),
        in_specs=[pl.BlockSpec((TM, d), lambda i: (i, 0)), pl.BlockSpec((1, d), lambda i: (0, 0))],
        out_specs=pl.BlockSpec((TM, d), lambda i: (i, 0)),
        out_shape=jax.ShapeDtypeStruct((t, d), out_dtype),
        compiler_params=_params(1),
        name="rmsnorm",
    )(x, g.reshape(1, d))


def _mm_body(x_ref, w_ref, o_ref):
    o_ref[...] = _dot(x_ref[...], w_ref[...]).astype(o_ref.dtype)


def _mm_res_body(x_ref, w_ref, r_ref, o_ref):
    o_ref[...] = r_ref[...] + _dot(x_ref[...], w_ref[...])


def _matmul(x, w, out_dtype, name):
    m, k = x.shape
    n = w.shape[1]
    return pl.pallas_call(
        _mm_body,
        grid=(n // TN, m // TM),
        in_specs=[pl.BlockSpec((TM, k), lambda j, i: (i, 0)), pl.BlockSpec((k, TN), lambda j, i: (0, j))],
        out_specs=pl.BlockSpec((TM, TN), lambda j, i: (i, j)),
        out_shape=jax.ShapeDtypeStruct((m, n), out_dtype),
        compiler_params=_params(2),
        name=name,
    )(x, w)


def _matmul_residual(x, w, res, name):
    m, k = x.shape
    n = w.shape[1]
    return pl.pallas_call(
        _mm_res_body,
        grid=(m // TM,),
        in_specs=[pl.BlockSpec((TM, k), lambda i: (i, 0)), pl.BlockSpec((k, n), lambda i: (0, 0)),
                  pl.BlockSpec((TM, n), lambda i: (i, 0))],
        out_specs=pl.BlockSpec((TM, n), lambda i: (i, 0)),
        out_shape=jax.ShapeDtypeStruct((m, n), F32),
        compiler_params=_params(1),
        name=name,
    )(x, w, res)


def _swiglu(g, u):
    return g * (1.0 / (1.0 + jnp.exp(-g))) * u


def _swiglu_body(x_ref, wg_ref, wu_ref, o_ref):
    x = x_ref[...]
    o_ref[...] = _swiglu(_dot(x, wg_ref[...]), _dot(x, wu_ref[...])).astype(o_ref.dtype)


def _swiglu_up(x, w_gate_up):
    m, k = x.shape
    n_panels = D_FF // TN_FF
    return pl.pallas_call(
        _swiglu_body,
        grid=(n_panels, m // TM),
        in_specs=[pl.BlockSpec((TM, k), lambda j, i: (i, 0)),
                  pl.BlockSpec((k, TN_FF), lambda j, i: (0, j)),
                  pl.BlockSpec((k, TN_FF), lambda j, i: (0, j + n_panels))],
        out_specs=pl.BlockSpec((TM, TN_FF), lambda j, i: (i, j)),
        out_shape=jax.ShapeDtypeStruct((m, D_FF), BF16),
        compiler_params=_params(2),
        name="swiglu_up",
    )(x, w_gate_up, w_gate_up)


def _dil_attn_body(q_ref, kc_ref, kp_ref, vc_ref, vp_ref, o_ref, lse_ref, *, dilation):
    n = pl.program_id(2)
    a = lax.broadcasted_iota(jnp.int32, (BLOCK, BLOCK), 0)
    m = lax.broadcasted_iota(jnp.int32, (BLOCK, BLOCK), 1)
    rel_c = a - m
    rel_p = rel_c + BLOCK
    valid_c = rel_c >= 0
    valid_p = (rel_p <= BLOCK) & (n > 0)
    dist_c = (rel_c * dilation).astype(F32)
    dist_p = (rel_p * dilation).astype(F32)
    first_head = m < HEAD_DIM
    neg_inf = F32(-jnp.inf)
    lse_tile = jnp.zeros((BLOCK, LANES), F32)
    for p in range(N_HEADS // 2):
        cols = slice(p * LANES, (p + 1) * LANES)
        q_pair = q_ref[0, :, cols]
        kc, kp = kc_ref[0, :, cols], kp_ref[0, :, cols]
        vc, vp = vc_ref[0, :, cols], vp_ref[0, :, cols]
        o_pair = None
        for hh in range(2):
            h = 2 * p + hh
            slope = ALIBI_SLOPES[h]
            q_h = jnp.where(first_head if hh == 0 else ~first_head, q_pair, jnp.zeros_like(q_pair))
            s_c = jnp.where(valid_c, _dot_nt(q_h, kc) * ATTN_SCALE - slope * dist_c, neg_inf)
            s_p = jnp.where(valid_p, _dot_nt(q_h, kp) * ATTN_SCALE - slope * dist_p, neg_inf)
            s_max = jnp.maximum(jnp.max(s_c, axis=1, keepdims=True), jnp.max(s_p, axis=1, keepdims=True))
            p_c = jnp.exp(s_c - s_max)
            p_p = jnp.exp(s_p - s_max)
            z = jnp.sum(p_c, axis=1, keepdims=True) + jnp.sum(p_p, axis=1, keepdims=True)
            o = (_dot(p_c.astype(BF16), vc) + _dot(p_p.astype(BF16), vp)) / z
            lse_tile = jnp.where(m == h, s_max + jnp.log(z), lse_tile)
            o_pair = o if hh == 0 else jnp.where(first_head, o_pair, o)
        o_ref[0, :, cols] = o_pair.astype(o_ref.dtype)
    lse_ref[0] = lse_tile


def _dilated_attention(proj, batch, seq, group):
    _, dilation = DILATED_GROUPS[group]
    sub_len = seq // dilation
    n_blocks = sub_len // BLOCK
    n_chunks = N_GROUPS * 3
    view = proj.reshape(batch, sub_len, dilation * n_chunks * D_ATTN)

    def chunk(which):
        return lambda b, r, n: (b, n, r * n_chunks + 3 * group + which)

    def chunk_prev(which):
        return lambda b, r, n: (b, jnp.maximum(n - 1, 0), r * n_chunks + 3 * group + which)

    blk = (1, BLOCK, D_ATTN)
    out, lse = pl.pallas_call(
        functools.partial(_dil_attn_body, dilation=dilation),
        grid=(batch, dilation, n_blocks),
        in_specs=[pl.BlockSpec(blk, chunk(0)), pl.BlockSpec(blk, chunk(1)), pl.BlockSpec(blk, chunk_prev(1)),
                  pl.BlockSpec(blk, chunk(2)), pl.BlockSpec(blk, chunk_prev(2))],
        out_specs=[pl.BlockSpec(blk, lambda b, r, n: (b, n, r)),
                   pl.BlockSpec((1, BLOCK, LANES), lambda b, r, n: (b, n, r))],
        out_shape=[jax.ShapeDtypeStruct((batch, sub_len, dilation * D_ATTN), BF16),
                   jax.ShapeDtypeStruct((batch, sub_len, dilation * LANES), F32)],
        compiler_params=_params(3),
        name=f"dilated_attn_g{group}",
    )(view, view, view, view, view)
    return out.reshape(batch * seq, D_ATTN), lse.reshape(batch * seq, LANES)


def _dil_out_body(o0_ref, o1_ref, o2_ref, l0_ref, l1_ref, l2_ref, e_ref, w_ref, h_ref, out_ref):
    lses = [l0_ref[...], l1_ref[...], l2_ref[...]]
    outs = [o0_ref, o1_ref, o2_ref]
    top = jnp.maximum(jnp.maximum(lses[0], lses[1]), lses[2])
    es = [jnp.exp(l - top) for l in lses]
    den = es[0] + es[1] + es[2]
    expand = e_ref[...]
    y = None
    for g in range(N_GROUPS):
        hi, lo = _split_bf16(es[g] / den)
        term = (_dot(hi, expand) + _dot(lo, expand)) * outs[g][...].astype(F32)
        y = term if y is None else y + term
    out_ref[...] = h_ref[...] + _dot(y.astype(BF16), w_ref[...])


def _dilated_out_proj(outs, lses, w_out, h):
    t = h.shape[0]
    tm = TM_EPILOGUE
    expand = (np.arange(LANES)[:, None] == (np.arange(D_ATTN)[None, :] // HEAD_DIM)).astype(np.float32)
    row = lambda w: pl.BlockSpec((tm, w), lambda i: (i, 0))
    full = lambda r, c: pl.BlockSpec((r, c), lambda i: (0, 0))
    return pl.pallas_call(
        _dil_out_body,
        grid=(t // tm,),
        in_specs=[row(D_ATTN)] * 3 + [row(LANES)] * 3 + [full(LANES, D_ATTN), full(D_ATTN, D_MODEL), row(D_MODEL)],
        out_specs=row(D_MODEL),
        out_shape=jax.ShapeDtypeStruct((t, D_MODEL), F32),
        compiler_params=_params(1),
        name="dilated_out_proj",
    )(*outs, *lses, jnp.asarray(expand, BF16), w_out, h)


def _sb_attn_body(q_ref, k_ref, v_ref, cum_ref, o_ref):
    i = pl.program_id(2)
    row = lax.broadcasted_iota(jnp.int32, (BLOCK, BLOCK), 0)
    lane = lax.broadcasted_iota(jnp.int32, (BLOCK, BLOCK), 1)
    strict = lane < row
    first_head = lane < HEAD_DIM
    q_pair = q_ref[0]
    cum = cum_ref[...]
    halves = []
    for hh in range(2):
        q_h = jnp.where(first_head if hh == 0 else ~first_head, q_pair, jnp.zeros_like(q_pair))

        def tile(j, carry, acc, diagonal):
            start = pl.multiple_of(j * BLOCK, BLOCK)
            k_j = k_ref[0, pl.ds(start, BLOCK), :]
            v_j = v_ref[0, pl.ds(start, BLOCK), :]
            z = _dot_nt(q_h, k_j) * ATTN_SCALE
            log_1mb = -(jnp.maximum(z, 0.0) + jnp.log(1.0 + jnp.exp(-jnp.abs(z))))
            log_beta = z + log_1mb
            if diagonal:
                log_1mb = jnp.where(strict, log_1mb, 0.0)
            hi, lo = _split_bf16(log_1mb)
            sums = _dot(hi, cum) + _dot(lo, cum)
            att = jnp.exp(log_beta + sums[:, :BLOCK] + carry)
            if diagonal:
                att = jnp.where(strict, att, 0.0)
            return carry + sums[:, BLOCK:], acc + _dot(att.astype(BF16), v_j)

        zeros = jnp.zeros((BLOCK, BLOCK), F32)
        carry, acc = tile(i, zeros, zeros, True)
        carry, acc = lax.fori_loop(0, i, lambda jj, c: tile(i - 1 - jj, c[0], c[1], False), (carry, acc))
        halves.append(acc)
    o_ref[0] = jnp.where(first_head, halves[0], halves[1]).astype(o_ref.dtype)


def _stick_breaking_attention(qkv, batch, seq):
    n_pairs = N_HEADS // 2
    view = qkv.reshape(batch, seq, 3 * D_ATTN)
    j = np.arange(BLOCK)
    cum = np.concatenate([(j[:, None] > j[None, :]), np.ones((BLOCK, BLOCK), bool)], axis=1).astype(np.float32)
    out = pl.pallas_call(
        _sb_attn_body,
        grid=(batch, n_pairs, seq // BLOCK),
        in_specs=[pl.BlockSpec((1, BLOCK, LANES), lambda b, p, i: (b, i, p)),
                  pl.BlockSpec((1, seq, LANES), lambda b, p, i: (b, 0, n_pairs + p)),
                  pl.BlockSpec((1, seq, LANES), lambda b, p, i: (b, 0, 2 * n_pairs + p)),
                  pl.BlockSpec((BLOCK, 2 * BLOCK), lambda b, p, i: (0, 0))],
        out_specs=pl.BlockSpec((1, BLOCK, LANES), lambda b, p, i: (b, i, p)),
        out_shape=jax.ShapeDtypeStruct((batch, seq, D_ATTN), BF16),
        compiler_params=_params(3),
        name="stick_breaking_attn",
    )(view, view, view, jnp.asarray(cum, BF16))
    return out.reshape(batch * seq, D_ATTN)


def _norm_router_body(h_ref, g_ref, wh_ref, wl_ref, hn_ref, gate_ref, idx_ref):
    y = _rmsnorm_rows(h_ref[...], g_ref[...])
    hn_ref[...] = y
    y_hi, y_lo = _split_bf16(y)
    w_hi, w_lo = wh_ref[...], wl_ref[...]
    logits = _dot(y_hi, w_hi) + _dot(y_hi, w_lo) + _dot(y_lo, w_hi)
    lane = lax.broadcasted_iota(jnp.int32, logits.shape, 1)
    neg_inf = F32(-jnp.inf)
    logits = jnp.where(lane < N_EXPERTS, logits, neg_inf)
    v1 = jnp.max(logits, axis=1, keepdims=True)
    i1 = jnp.min(jnp.where(logits == v1, lane, LANES), axis=1, keepdims=True)
    rest = jnp.where(lane == i1, neg_inf, logits)
    v2 = jnp.max(rest, axis=1, keepdims=True)
    i2 = jnp.min(jnp.where(rest == v2, lane, LANES), axis=1, keepdims=True)
    e2 = jnp.exp(v2 - v1)
    w1 = 1.0 / (1.0 + e2)
    w2 = e2 * w1
    gate_ref[...] = jnp.where(lane == 0, w1, jnp.where(lane == 1, w2, 0.0))
    idx_ref[...] = jnp.where(lane == 0, i1, jnp.where(lane == 1, i2, 0))


def _norm_router(h, g, w_router):
    t, d = h.shape
    tm = TM_EPILOGUE
    w_pad = jnp.zeros((d, LANES), F32).at[:, :N_EXPERTS].set(w_router)
    w_hi = w_pad.astype(BF16)
    w_lo = (w_pad - w_hi.astype(F32)).astype(BF16)
    row = lambda w: pl.BlockSpec((tm, w), lambda i: (i, 0))
    full = lambda r, c: pl.BlockSpec((r, c), lambda i: (0, 0))
    return pl.pallas_call(
        _norm_router_body,
        grid=(t // tm,),
        in_specs=[row(d), full(1, d), full(d, LANES), full(d, LANES)],
        out_specs=[row(d), row(LANES), row(LANES)],
        out_shape=[jax.ShapeDtypeStruct((t, d), F32), jax.ShapeDtypeStruct((t, LANES), F32),
                   jax.ShapeDtypeStruct((t, LANES), jnp.int32)],
        compiler_params=_params(1),
        name="norm_router",
    )(h, g.reshape(1, d), w_hi, w_lo)


def _route_layout(idx, n_slots):
    n_tiles = n_slots // TM_GROUP
    flat = idx[:, :TOP_K].reshape(-1)
    onehot = (flat[:, None] == jnp.arange(N_EXPERTS, dtype=jnp.int32)[None, :]).astype(jnp.int32)
    csum = jnp.cumsum(onehot, axis=0)
    rank = jnp.sum(csum * onehot, axis=1) - 1
    counts = csum[-1]
    padded = ((counts + TM_GROUP - 1) // TM_GROUP) * TM_GROUP
    pad_end = jnp.cumsum(padded)
    dest = (pad_end - padded)[flat] + rank
    tokens = jnp.arange(flat.shape[0], dtype=jnp.int32) // TOP_K
    src_tok = jnp.zeros((n_slots,), jnp.int32).at[dest].set(tokens)
    tile_start = jnp.arange(n_tiles, dtype=jnp.int32) * TM_GROUP
    tile_expert = jnp.minimum(jnp.sum((tile_start[:, None] >= pad_end[None, :]).astype(jnp.int32), axis=1),
                              N_EXPERTS - 1)
    n_used = (pad_end[-1:] // TM_GROUP).astype(jnp.int32)
    return src_tok, dest.reshape(-1, TOP_K), tile_expert, n_used


def _row_gather(idx_ref, n_rows, src_hbm, dst_for_row, sem):
    def copy(r):
        return pltpu.make_async_copy(src_hbm.at[pl.ds(idx_ref[0, 0, r], 1)], dst_for_row(r), sem)

    def start(r, c):
        copy(r).start()
        return c

    def wait(r, c):
        copy(r).wait()
        return c

    lax.fori_loop(0, n_rows, start, 0)
    lax.fori_loop(0, n_rows, wait, 0)


def _gather_body(src_ref, x_hbm, o_ref, buf, sem):
    _row_gather(src_ref, TM_GROUP, x_hbm, lambda r: buf.at[pl.ds(r, 1)], sem)
    o_ref[...] = buf[...].astype(o_ref.dtype)


def _gather_rows(x, src_tok):
    d = x.shape[1]
    n_slots = src_tok.shape[0]
    n_tiles = n_slots // TM_GROUP
    return pl.pallas_call(
        _gather_body,
        grid=(n_tiles,),
        in_specs=[pl.BlockSpec((1, 1, TM_GROUP), lambda i: (i, 0, 0), memory_space=pltpu.SMEM),
                  pl.BlockSpec(memory_space=pl.ANY)],
        out_specs=pl.BlockSpec((TM_GROUP, d), lambda i: (i, 0)),
        out_shape=jax.ShapeDtypeStruct((n_slots, d), BF16),
        scratch_shapes=[pltpu.VMEM((TM_GROUP, d), F32), pltpu.SemaphoreType.DMA(())],
        compiler_params=_params(1),
        name="moe_gather",
    )(src_tok.reshape(n_tiles, 1, TM_GROUP), x)


def _group_swiglu_body(te_ref, nu_ref, x_ref, wg_ref, wu_ref, o_ref):
    used = pl.program_id(1) < nu_ref[0]

    @pl.when(used)
    def _():
        x = x_ref[...]
        o_ref[...] = _swiglu(_dot(x, wg_ref[0]), _dot(x, wu_ref[0])).astype(o_ref.dtype)

    @pl.when(jnp.logical_not(used))
    def _():
        o_ref[...] = jnp.zeros_like(o_ref)


def _group_down_body(te_ref, nu_ref, x_ref, w_ref, o_ref):
    used = pl.program_id(0) < nu_ref[0]

    @pl.when(used)
    def _():
        o_ref[...] = _dot(x_ref[...], w_ref[0])

    @pl.when(jnp.logical_not(used))
    def _():
        o_ref[...] = jnp.zeros_like(o_ref)


def _group_swiglu_up(xs, w_gate_up, tile_expert, n_used):
    n_slots, k = xs.shape
    n_tiles = n_slots // TM_GROUP
    n_panels = D_FF // TN_FF
    last = lambda t, nu: jnp.minimum(t, nu[0] - 1)
    grid_spec = pltpu.PrefetchScalarGridSpec(
        num_scalar_prefetch=2,
        grid=(n_panels, n_tiles),
        in_specs=[pl.BlockSpec((TM_GROUP, k), lambda j, t, te, nu: (last(t, nu), 0)),
                  pl.BlockSpec((1, k, TN_FF), lambda j, t, te, nu: (te[last(t, nu)], 0, j)),
                  pl.BlockSpec((1, k, TN_FF), lambda j, t, te, nu: (te[last(t, nu)], 0, j + n_panels))],
        out_specs=pl.BlockSpec((TM_GROUP, TN_FF), lambda j, t, te, nu: (t, j)),
    )
    return pl.pallas_call(
        _group_swiglu_body,
        grid_spec=grid_spec,
        out_shape=jax.ShapeDtypeStruct((n_slots, D_FF), BF16),
        compiler_params=_params(2),
        name="moe_swiglu_up",
    )(tile_expert, n_used, xs, w_gate_up, w_gate_up)


def _group_down(act, w_down, tile_expert, n_used):
    n_slots, k = act.shape
    n = w_down.shape[2]
    n_tiles = n_slots // TM_GROUP
    last = lambda t, nu: jnp.minimum(t, nu[0] - 1)
    grid_spec = pltpu.PrefetchScalarGridSpec(
        num_scalar_prefetch=2,
        grid=(n_tiles,),
        in_specs=[pl.BlockSpec((TM_GROUP, k), lambda t, te, nu: (last(t, nu), 0)),
                  pl.BlockSpec((1, k, n), lambda t, te, nu: (te[last(t, nu)], 0, 0))],
        out_specs=pl.BlockSpec((TM_GROUP, n), lambda t, te, nu: (t, 0)),
    )
    return pl.pallas_call(
        _group_down_body,
        grid_spec=grid_spec,
        out_shape=jax.ShapeDtypeStruct((n_slots, n), F32),
        compiler_params=_params(1),
        name="moe_down",
    )(tile_expert, n_used, act, w_down)


def _moe_combine_body(slot_ref, gate_ref, h_ref, y_hbm, out_ref, buf, sem):
    tm = TM_COMBINE
    _row_gather(slot_ref, TOP_K * tm, y_hbm,
                lambda e: buf.at[e & 1, pl.ds(lax.shift_right_logical(e, 1), 1)], sem)
    gates = gate_ref[...]
    out_ref[...] = h_ref[...] + gates[:, 0:1] * buf[0] + gates[:, 1:2] * buf[1]


def _moe_combine(h, y_slots, slots, gates):
    t, d = h.shape
    tm = TM_COMBINE
    n_tiles = t // tm
    return pl.pallas_call(
        _moe_combine_body,
        grid=(n_tiles,),
        in_specs=[pl.BlockSpec((1, 1, TOP_K * tm), lambda i: (i, 0, 0), memory_space=pltpu.SMEM),
                  pl.BlockSpec((tm, LANES), lambda i: (i, 0)),
                  pl.BlockSpec((tm, d), lambda i: (i, 0)),
                  pl.BlockSpec(memory_space=pl.ANY)],
        out_specs=pl.BlockSpec((tm, d), lambda i: (i, 0)),
        out_shape=jax.ShapeDtypeStruct((t, d), F32),
        scratch_shapes=[pltpu.VMEM((TOP_K, tm, d), F32), pltpu.SemaphoreType.DMA(())],
        compiler_params=_params(1),
        name="moe_combine",
    )(slots.reshape(n_tiles, 1, TOP_K * tm), gates, h, y_slots)


def _moe_layer(h, g, w_router, w_gate_up, w_down):
    t = h.shape[0]
    n_slots = t * TOP_K + N_EXPERTS * TM_GROUP
    hn, gates, idx = _norm_router(h, g, w_router)
    src_tok, slots, tile_expert, n_used = _route_layout(idx, n_slots)
    xs = _gather_rows(hn, src_tok)
    act = _group_swiglu_up(xs, w_gate_up, tile_expert, n_used)
    y_slots = _group_down(act, w_down, tile_expert, n_used)
    return _moe_combine(h, y_slots, slots, gates)


def kernel(x, mix_norm, ffn_norm, dil_w_in, dil_w_out, sb_w_qkv, sb_w_out, ffn_w_gate_up, ffn_w_down,
           moe_w_router, moe_w_gate_up, moe_w_down, final_norm):
    batch, seq, d = x.shape
    depth = mix_norm.shape[0]
    h = x.reshape(batch * seq, d)
    for i in range(depth):
        j = i // 2
        hn = _rmsnorm(h, mix_norm[i], BF16)
        if i % 2 == 0:
            proj = _matmul(hn, dil_w_in[j].astype(BF16), BF16, "dilated_in_proj")
            parts = [_dilated_attention(proj, batch, seq, g) for g in range(N_GROUPS)]
            h = _dilated_out_proj([p[0] for p in parts], [p[1] for p in parts], dil_w_out[j].astype(BF16), h)
            hn = _rmsnorm(h, ffn_norm[i], BF16)
            act = _swiglu_up(hn, ffn_w_gate_up[j].astype(BF16))
            h = _matmul_residual(act, ffn_w_down[j].astype(BF16), h, "ffn_down")
        else:
            qkv = _matmul(hn, sb_w_qkv[j].astype(BF16), BF16, "sb_qkv_proj")
            y = _stick_breaking_attention(qkv, batch, seq)
            h = _matmul_residual(y, sb_w_out[j].astype(BF16), h, "sb_out_proj")
            h = _moe_layer(h, ffn_norm[i], moe_w_router[j], moe_w_gate_up[j].astype(BF16),
                           moe_w_down[j].astype(BF16))
    return _rmsnorm(h, final_norm, F32).reshape(batch, seq, d)
```

```python
import functools

import numpy as np
import jax
import jax.numpy as jnp
from jax import lax
from jax.experimental import pallas as pl
from jax.experimental.pallas import tpu as pltpu

D_MODEL = 1024
N_HEADS = 16
HEAD_DIM = 64
D_ATTN = N_HEADS * HEAD_DIM
DILATED_GROUPS = ((128, 1), (512, 4), (2048, 16))
N_GROUPS = len(DILATED_GROUPS)
BLOCK = 128
D_FF = 3584
N_EXPERTS = 8
TOP_K = 2
RMS_EPS = 1e-6
ATTN_SCALE = 0.125
NEG_LOG2E = -float(np.float32(np.log2(np.e)))

LANES = 128
VMEM_LIMIT_BYTES = 56 * 1024 * 1024

TM = 512
TN = 1024
TN_FF = 512
TM_GROUP = 512
TM_COMBINE = 256
TM_EPILOGUE = 256
SB_CHUNK = 256
DIL_PAIRS_PER_STAGE = 4
ROW_DMA_UNROLL = 8

F32 = jnp.float32
BF16 = jnp.bfloat16

ALIBI_SLOPES = tuple(float(np.float32(2.0 ** (-8.0 * (h + 1) / N_HEADS))) for h in range(N_HEADS))


def _params(n_axes):
    return pltpu.CompilerParams(dimension_semantics=("arbitrary",) * n_axes, vmem_limit_bytes=VMEM_LIMIT_BYTES)


def _dot(a, b):
    return jnp.dot(a, b, preferred_element_type=F32)


def _dot_nt(a, b):
    return lax.dot_general(a, b, (((1,), (1,)), ((), ())), preferred_element_type=F32)


def _split_bf16(x):
    hi = x.astype(BF16)
    lo = (x - hi.astype(F32)).astype(BF16)
    return hi, lo


def _rmsnorm_rows(x, g):
    return x * lax.rsqrt(jnp.mean(x * x, axis=-1, keepdims=True) + RMS_EPS) * g


def _rmsnorm_body(x_ref, g_ref, o_ref):
    o_ref[...] = _rmsnorm_rows(x_ref[...], g_ref[...]).astype(o_ref.dtype)


def _rmsnorm(x, g, out_dtype):
    t, d = x.shape
    return pl.pallas_call(
        _rmsnorm_body,
        grid=(t // TM,),
        in_specs=[pl.BlockSpec((TM, d), lambda i: (i, 0)), pl.BlockSpec((1, d), lambda i: (0, 0))],
        out_specs=pl.BlockSpec((TM, d), lambda i: (i, 0)),
        out_shape=jax.ShapeDtypeStruct((t, d), out_dtype),
        compiler_params=_params(1),
        name="rmsnorm",
    )(x, g.reshape(1, d))


def _rmsnorm_residue_body(x_ref, g_ref, *refs):
    o_refs, lanes_ref = refs[:-1], refs[-1]
    y = _rmsnorm_rows(x_ref[0], g_ref[...])
    n_chunks = y.shape[1] // LANES
    for c in range(n_chunks):
        lanes_ref[c] = y[:, c * LANES:(c + 1) * LANES]
    for o_ref, (_, dilation) in zip(o_refs, DILATED_GROUPS):
        if dilation == 1:
            o_ref[0, 0] = y.astype(o_ref.dtype)
            continue
        rows = TM // dilation
        for r in range(dilation):
            for c in range(n_chunks):
                picked = lanes_ref[c, pl.ds(r, rows, stride=dilation), :]
                o_ref[0, r, :, c * LANES:(c + 1) * LANES] = picked.astype(o_ref.dtype)


def _rmsnorm_by_residue(x, g, batch, seq):
    d_model = x.shape[1]
    tiles = seq // TM
    out_specs, out_shape = [], []
    for _, dilation in DILATED_GROUPS:
        out_specs.append(pl.BlockSpec((1, dilation, TM // dilation, d_model), lambda b, i: (b, 0, i, 0)))
        out_shape.append(jax.ShapeDtypeStruct((batch, dilation, seq // dilation, d_model), BF16))
    outs = pl.pallas_call(
        _rmsnorm_residue_body,
        grid=(batch, tiles),
        in_specs=[pl.BlockSpec((1, TM, d_model), lambda b, i: (b, i, 0)),
                  pl.BlockSpec((1, d_model), lambda b, i: (0, 0))],
        out_specs=out_specs,
        out_shape=out_shape,
        scratch_shapes=[pltpu.VMEM((d_model // LANES, TM, LANES), F32)],
        compiler_params=_params(2),
        name="rmsnorm_by_residue",
    )(x.reshape(batch, seq, d_model), g.reshape(1, d_model))
    return [o.reshape(batch * seq, d_model) for o in outs]


def _mm_body(x_ref, w_ref, o_ref):
    o_ref[...] = _dot(x_ref[...], w_ref[...]).astype(o_ref.dtype)


def _mm_res_body(x_ref, w_ref, r_ref, o_ref):
    o_ref[...] = r_ref[...] + _dot(x_ref[...], w_ref[...])


def _matmul(x, w, out_dtype, name, first_col=0, n_cols=None):
    m, k = x.shape
    n = w.shape[1] if n_cols is None else n_cols
    col0 = first_col // TN
    return pl.pallas_call(
        _mm_body,
        grid=(n // TN, m // TM),
        in_specs=[pl.BlockSpec((TM, k), lambda j, i: (i, 0)), pl.BlockSpec((k, TN), lambda j, i: (0, col0 + j))],
        out_specs=pl.BlockSpec((TM, TN), lambda j, i: (i, j)),
        out_shape=jax.ShapeDtypeStruct((m, n), out_dtype),
        compiler_params=_params(2),
        name=name,
    )(x, w)


def _matmul_residual(x, w, res, name):
    m, k = x.shape
    n = w.shape[1]
    return pl.pallas_call(
        _mm_res_body,
        grid=(m // TM,),
        in_specs=[pl.BlockSpec((TM, k), lambda i: (i, 0)), pl.BlockSpec((k, n), lambda i: (0, 0)),
                  pl.BlockSpec((TM, n), lambda i: (i, 0))],
        out_specs=pl.BlockSpec((TM, n), lambda i: (i, 0)),
        out_shape=jax.ShapeDtypeStruct((m, n), F32),
        compiler_params=_params(1),
        name=name,
    )(x, w, res)


def _swiglu(g, u):
    return g * (1.0 / (1.0 + jnp.exp(-g))) * u


def _swiglu_body(x_ref, wg_ref, wu_ref, o_ref):
    x = x_ref[...]
    o_ref[...] = _swiglu(_dot(x, wg_ref[...]), _dot(x, wu_ref[...])).astype(o_ref.dtype)


def _swiglu_up(x, w_gate_up):
    m, k = x.shape
    n_panels = D_FF // TN_FF
    return pl.pallas_call(
        _swiglu_body,
        grid=(n_panels, m // TM),
        in_specs=[pl.BlockSpec((TM, k), lambda j, i: (i, 0)),
                  pl.BlockSpec((k, TN_FF), lambda j, i: (0, j)),
                  pl.BlockSpec((k, TN_FF), lambda j, i: (0, j + n_panels))],
        out_specs=pl.BlockSpec((TM, TN_FF), lambda j, i: (i, j)),
        out_shape=jax.ShapeDtypeStruct((m, D_FF), BF16),
        compiler_params=_params(2),
        name="swiglu_up",
    )(x, w_gate_up, w_gate_up)


def _alibi_window_bias(dilation):
    a = np.arange(BLOCK)[:, None]
    m = np.arange(2 * BLOCK)[None, :]
    rel = BLOCK + a - m
    in_window = (rel >= 0) & (rel <= BLOCK)
    slopes = np.asarray(ALIBI_SLOPES, np.float32)[:, None, None]
    bias = -slopes * (rel * dilation).astype(np.float32)[None]
    later = np.where(in_window[None], bias, -np.inf)
    first = np.where((m >= BLOCK)[None], later, -np.inf)
    return np.stack([first, later]).astype(np.float32)


def _dil_attn_body(q_ref, kc_ref, kp_ref, vc_ref, vp_ref, bias_ref, o_ref, lse_ref):
    n_keys = 2 * BLOCK
    lane = lax.broadcasted_iota(jnp.int32, (BLOCK, LANES), 1)
    first_head = lane < HEAD_DIM
    first_head_keys = lax.broadcasted_iota(jnp.int32, (n_keys, LANES), 1) < HEAD_DIM
    lse_tile = jnp.zeros((BLOCK, LANES), F32)

    def by_head(pair_rows):
        zero = jnp.zeros_like(pair_rows)
        return jnp.concatenate([jnp.where(first_head_keys, pair_rows, zero),
                                jnp.where(first_head_keys, zero, pair_rows)], axis=0)

    for first_pair in range(0, N_HEADS // 2, DIL_PAIRS_PER_STAGE):
        pairs = range(first_pair, first_pair + DIL_PAIRS_PER_STAGE)
        cols = {p: slice(p * LANES, (p + 1) * LANES) for p in pairs}
        scores = {p: _dot_nt(q_ref[:, cols[p]] * ATTN_SCALE,
                             by_head(jnp.concatenate([kp_ref[:, cols[p]], kc_ref[:, cols[p]]], axis=0)))
                  for p in pairs}
        probs, norms = {}, {}
        for p in pairs:
            pair_probs = []
            for hh in range(2):
                h = 2 * p + hh
                s = scores[p][:, hh * n_keys:(hh + 1) * n_keys] + bias_ref[0, h]
                s_max = jnp.max(s, axis=1, keepdims=True)
                e = jnp.exp(s - s_max)
                z = jnp.sum(e, axis=1, keepdims=True)
                lse_tile = jnp.where(lane == h, s_max + jnp.log(z), lse_tile)
                pair_probs.append(e.astype(BF16))
                norms[h] = z
            probs[p] = jnp.concatenate(pair_probs, axis=1)
        for p in pairs:
            v_heads = by_head(jnp.concatenate([vp_ref[:, cols[p]], vc_ref[:, cols[p]]], axis=0))
            o_pair = _dot(probs[p], v_heads) / jnp.where(first_head, norms[2 * p], norms[2 * p + 1])
            o_ref[0, :, cols[p]] = o_pair.astype(o_ref.dtype)
    lse_ref[0] = lse_tile


def _dilated_attention(proj, batch, seq, group):
    _, dilation = DILATED_GROUPS[group]
    sub_len = seq // dilation
    n_blocks = sub_len // BLOCK

    def block(which, back):
        return lambda b, r, n: ((b * dilation + r) * n_blocks + jnp.maximum(n - back, 0), which)

    blk = (BLOCK, D_ATTN)
    bias = jnp.asarray(_alibi_window_bias(dilation))
    out, lse = pl.pallas_call(
        _dil_attn_body,
        grid=(batch, dilation, n_blocks),
        in_specs=[pl.BlockSpec(blk, block(0, 0)), pl.BlockSpec(blk, block(1, 0)), pl.BlockSpec(blk, block(1, 1)),
                  pl.BlockSpec(blk, block(2, 0)), pl.BlockSpec(blk, block(2, 1)),
                  pl.BlockSpec((1,) + bias.shape[1:], lambda b, r, n: (jnp.minimum(n, 1), 0, 0, 0))],
        out_specs=[pl.BlockSpec((1, BLOCK, D_ATTN), lambda b, r, n: (b, n, r)),
                   pl.BlockSpec((1, BLOCK, LANES), lambda b, r, n: (b, n, r))],
        out_shape=[jax.ShapeDtypeStruct((batch, sub_len, dilation * D_ATTN), BF16),
                   jax.ShapeDtypeStruct((batch, sub_len, dilation * LANES), F32)],
        compiler_params=_params(3),
        name=f"dilated_attn_g{group}",
    )(proj, proj, proj, proj, proj, bias)
    return out.reshape(batch * seq, D_ATTN), lse.reshape(batch * seq, LANES)


def _dil_out_body(o0_ref, o1_ref, o2_ref, l0_ref, l1_ref, l2_ref, e_ref, w_ref, h_ref, out_ref):
    lses = [l0_ref[...], l1_ref[...], l2_ref[...]]
    outs = [o0_ref, o1_ref, o2_ref]
    top = jnp.maximum(jnp.maximum(lses[0], lses[1]), lses[2])
    es = [jnp.exp(l - top) for l in lses]
    den = es[0] + es[1] + es[2]
    expand = e_ref[...]
    y = None
    for g in range(N_GROUPS):
        hi, lo = _split_bf16(es[g] / den)
        term = (_dot(hi, expand) + _dot(lo, expand)) * outs[g][...].astype(F32)
        y = term if y is None else y + term
    out_ref[...] = h_ref[...] + _dot(y.astype(BF16), w_ref[...])


def _dilated_out_proj(outs, lses, w_out, h):
    t = h.shape[0]
    tm = TM_EPILOGUE
    expand = (np.arange(LANES)[:, None] == (np.arange(D_ATTN)[None, :] // HEAD_DIM)).astype(np.float32)
    row = lambda w: pl.BlockSpec((tm, w), lambda i: (i, 0))
    full = lambda r, c: pl.BlockSpec((r, c), lambda i: (0, 0))
    return pl.pallas_call(
        _dil_out_body,
        grid=(t // tm,),
        in_specs=[row(D_ATTN)] * 3 + [row(LANES)] * 3 + [full(LANES, D_ATTN), full(D_ATTN, D_MODEL), row(D_MODEL)],
        out_specs=row(D_MODEL),
        out_shape=jax.ShapeDtypeStruct((t, D_MODEL), F32),
        compiler_params=_params(1),
        name="dilated_out_proj",
    )(*outs, *lses, jnp.asarray(expand, BF16), w_out, h)


def _dilated_mixer(h, g, w_in, w_out, batch, seq):
    w_in = w_in.astype(BF16)
    parts = []
    for group, hn in enumerate(_rmsnorm_by_residue(h, g, batch, seq)):
        proj = _matmul(hn, w_in, BF16, f"dilated_in_proj_g{group}", first_col=group * 3 * D_ATTN, n_cols=3 * D_ATTN)
        parts.append(_dilated_attention(proj, batch, seq, group))
    return _dilated_out_proj([p[0] for p in parts], [p[1] for p in parts], w_out.astype(BF16), h)


def _sb_attn_body(q_ref, k_ref, v_ref, tri_ref, o_ref):
    i = pl.program_id(2)
    row = lax.broadcasted_iota(jnp.int32, (SB_CHUNK, SB_CHUNK), 0)
    col = lax.broadcasted_iota(jnp.int32, (SB_CHUNK, SB_CHUNK), 1)
    strict = col < row
    first_head = lax.broadcasted_iota(jnp.int32, (SB_CHUNK, LANES), 1) < HEAD_DIM
    head_lanes = (first_head, ~first_head)
    q_pair = q_ref[0] * ATTN_SCALE
    q_heads = [jnp.where(sel, q_pair, jnp.zeros_like(q_pair)) for sel in head_lanes]
    tri = tri_ref[...]
    heads = range(2)

    def chunk(j, state, diagonal):
        carries, acc = state
        start = pl.multiple_of(j * SB_CHUNK, SB_CHUNK)
        k_j = k_ref[0, pl.ds(start, SB_CHUNK), :]
        v_j = v_ref[0, pl.ds(start, SB_CHUNK), :]
        zs = [_dot_nt(q_heads[hh], k_j) for hh in heads]
        if diagonal:
            zs = [jnp.where(strict, z, -1e30) for z in zs]
        sps = [jnp.maximum(z, 0.0) + jnp.log(1.0 + jnp.exp2(jnp.abs(z) * NEG_LOG2E)) for z in zs]
        splits = [_split_bf16(sp) for sp in sps]
        suffixes = [_dot(hi, tri) + _dot(lo, tri) for hi, lo in splits]
        atts = [jnp.exp(zs[hh] + suffixes[hh] + carries[hh]) for hh in heads]
        for hh in heads:
            v_h = jnp.where(head_lanes[hh], v_j, jnp.zeros_like(v_j))
            acc = acc + _dot(atts[hh].astype(BF16), v_h)
        return tuple(carries[hh] + suffixes[hh][:, 0:1] for hh in heads), acc

    zero_carry = jnp.zeros((SB_CHUNK, 1), F32)
    state = chunk(i, ((zero_carry, zero_carry), jnp.zeros((SB_CHUNK, LANES), F32)), True)
    _, acc = lax.fori_loop(0, i, lambda jj, s: chunk(i - 1 - jj, s, False), state)
    o_ref[0] = acc.astype(o_ref.dtype)


def _stick_breaking_attention(qkv, batch, seq):
    n_pairs = N_HEADS // 2
    view = qkv.reshape(batch, seq, 3 * D_ATTN)
    j = np.arange(SB_CHUNK)
    tri = -(j[:, None] >= j[None, :]).astype(np.float32)
    out = pl.pallas_call(
        _sb_attn_body,
        grid=(batch, n_pairs, seq // SB_CHUNK),
        in_specs=[pl.BlockSpec((1, SB_CHUNK, LANES), lambda b, p, i: (b, i, p)),
                  pl.BlockSpec((1, seq, LANES), lambda b, p, i: (b, 0, n_pairs + p)),
                  pl.BlockSpec((1, seq, LANES), lambda b, p, i: (b, 0, 2 * n_pairs + p)),
                  pl.BlockSpec((SB_CHUNK, SB_CHUNK), lambda b, p, i: (0, 0))],
        out_specs=pl.BlockSpec((1, SB_CHUNK, LANES), lambda b, p, i: (b, i, p)),
        out_shape=jax.ShapeDtypeStruct((batch, seq, D_ATTN), BF16),
        compiler_params=_params(3),
        name="stick_breaking_attn",
    )(view, view, view, jnp.asarray(tri, BF16))
    return out.reshape(batch * seq, D_ATTN)


def _norm_router_body(h_ref, g_ref, wh_ref, wl_ref, hn_ref, gate_ref, idx_ref):
    y = _rmsnorm_rows(h_ref[...], g_ref[...])
    hn_ref[...] = y
    y_hi, y_lo = _split_bf16(y)
    w_hi, w_lo = wh_ref[...], wl_ref[...]
    logits = _dot(y_hi, w_hi) + _dot(y_hi, w_lo) + _dot(y_lo, w_hi)
    lane = lax.broadcasted_iota(jnp.int32, logits.shape, 1)
    neg_inf = F32(-jnp.inf)
    logits = jnp.where(lane < N_EXPERTS, logits, neg_inf)
    v1 = jnp.max(logits, axis=1, keepdims=True)
    i1 = jnp.min(jnp.where(logits == v1, lane, LANES), axis=1, keepdims=True)
    rest = jnp.where(lane == i1, neg_inf, logits)
    v2 = jnp.max(rest, axis=1, keepdims=True)
    i2 = jnp.min(jnp.where(rest == v2, lane, LANES), axis=1, keepdims=True)
    e2 = jnp.exp(v2 - v1)
    w1 = 1.0 / (1.0 + e2)
    w2 = e2 * w1
    gate_ref[...] = jnp.where(lane == 0, w1, jnp.where(lane == 1, w2, 0.0))
    idx_ref[...] = jnp.where(lane == 0, i1, jnp.where(lane == 1, i2, 0))


def _norm_router(h, g, w_router):
    t, d = h.shape
    tm = TM_EPILOGUE
    w_pad = jnp.zeros((d, LANES), F32).at[:, :N_EXPERTS].set(w_router)
    w_hi = w_pad.astype(BF16)
    w_lo = (w_pad - w_hi.astype(F32)).astype(BF16)
    row = lambda w: pl.BlockSpec((tm, w), lambda i: (i, 0))
    full = lambda r, c: pl.BlockSpec((r, c), lambda i: (0, 0))
    return pl.pallas_call(
        _norm_router_body,
        grid=(t // tm,),
        in_specs=[row(d), full(1, d), full(d, LANES), full(d, LANES)],
        out_specs=[row(d), row(LANES), row(LANES)],
        out_shape=[jax.ShapeDtypeStruct((t, d), F32), jax.ShapeDtypeStruct((t, LANES), F32),
                   jax.ShapeDtypeStruct((t, LANES), jnp.int32)],
        compiler_params=_params(1),
        name="norm_router",
    )(h, g.reshape(1, d), w_hi, w_lo)


def _route_layout(idx, n_slots):
    n_tiles = n_slots // TM_GROUP
    flat = idx[:, :TOP_K].reshape(-1)
    onehot = (flat[:, None] == jnp.arange(N_EXPERTS, dtype=jnp.int32)[None, :]).astype(jnp.int32)
    csum = jnp.cumsum(onehot, axis=0)
    rank = jnp.sum(csum * onehot, axis=1) - 1
    counts = csum[-1]
    padded = ((counts + TM_GROUP - 1) // TM_GROUP) * TM_GROUP
    pad_end = jnp.cumsum(padded)
    dest = (pad_end - padded)[flat] + rank
    tokens = jnp.arange(flat.shape[0], dtype=jnp.int32) // TOP_K
    src_tok = jnp.zeros((n_slots,), jnp.int32).at[dest].set(tokens)
    tile_start = jnp.arange(n_tiles, dtype=jnp.int32) * TM_GROUP
    tile_expert = jnp.minimum(jnp.sum((tile_start[:, None] >= pad_end[None, :]).astype(jnp.int32), axis=1),
                              N_EXPERTS - 1)
    n_used = (pad_end[-1:] // TM_GROUP).astype(jnp.int32)
    return src_tok, dest.reshape(-1, TOP_K), tile_expert, n_used


def _row_gather(idx_ref, n_rows, src_hbm, dst_for_row, sem):
    def copy(r):
        return pltpu.make_async_copy(src_hbm.at[pl.ds(idx_ref[0, 0, r], 1)], dst_for_row(r), sem)

    def start(g, c):
        for u in range(ROW_DMA_UNROLL):
            copy(g * ROW_DMA_UNROLL + u).start(priority=u % 2)
        return c

    def wait(g, c):
        for u in range(ROW_DMA_UNROLL):
            copy(g * ROW_DMA_UNROLL + u).wait()
        return c

    lax.fori_loop(0, n_rows // ROW_DMA_UNROLL, start, 0)
    lax.fori_loop(0, n_rows // ROW_DMA_UNROLL, wait, 0)


def _gather_body(src_ref, x_hbm, o_ref, buf, sem):
    _row_gather(src_ref, TM_GROUP, x_hbm, lambda r: buf.at[pl.ds(r, 1)], sem)
    o_ref[...] = buf[...].astype(o_ref.dtype)


def _gather_rows(x, src_tok):
    d = x.shape[1]
    n_slots = src_tok.shape[0]
    n_tiles = n_slots // TM_GROUP
    return pl.pallas_call(
        _gather_body,
        grid=(n_tiles,),
        in_specs=[pl.BlockSpec((1, 1, TM_GROUP), lambda i: (i, 0, 0), memory_space=pltpu.SMEM),
                  pl.BlockSpec(memory_space=pl.ANY)],
        out_specs=pl.BlockSpec((TM_GROUP, d), lambda i: (i, 0)),
        out_shape=jax.ShapeDtypeStruct((n_slots, d), BF16),
        scratch_shapes=[pltpu.VMEM((TM_GROUP, d), F32), pltpu.SemaphoreType.DMA(())],
        compiler_params=_params(1),
        name="moe_gather",
    )(src_tok.reshape(n_tiles, 1, TM_GROUP), x)


def _group_swiglu_body(te_ref, nu_ref, x_ref, wg_ref, wu_ref, o_ref):
    used = pl.program_id(1) < nu_ref[0]

    @pl.when(used)
    def _():
        x = x_ref[...]
        o_ref[...] = _swiglu(_dot(x, wg_ref[0]), _dot(x, wu_ref[0])).astype(o_ref.dtype)

    @pl.when(jnp.logical_not(used))
    def _():
        o_ref[...] = jnp.zeros_like(o_ref)


def _group_down_body(te_ref, nu_ref, x_ref, w_ref, o_ref):
    used = pl.program_id(0) < nu_ref[0]

    @pl.when(used)
    def _():
        o_ref[...] = _dot(x_ref[...], w_ref[0])

    @pl.when(jnp.logical_not(used))
    def _():
        o_ref[...] = jnp.zeros_like(o_ref)


def _group_swiglu_up(xs, w_gate_up, tile_expert, n_used):
    n_slots, k = xs.shape
    n_tiles = n_slots // TM_GROUP
    n_panels = D_FF // TN_FF
    last = lambda t, nu: jnp.minimum(t, nu[0] - 1)
    grid_spec = pltpu.PrefetchScalarGridSpec(
        num_scalar_prefetch=2,
        grid=(n_panels, n_tiles),
        in_specs=[pl.BlockSpec((TM_GROUP, k), lambda j, t, te, nu: (last(t, nu), 0)),
                  pl.BlockSpec((1, k, TN_FF), lambda j, t, te, nu: (te[last(t, nu)], 0, j)),
                  pl.BlockSpec((1, k, TN_FF), lambda j, t, te, nu: (te[last(t, nu)], 0, j + n_panels))],
        out_specs=pl.BlockSpec((TM_GROUP, TN_FF), lambda j, t, te, nu: (t, j)),
    )
    return pl.pallas_call(
        _group_swiglu_body,
        grid_spec=grid_spec,
        out_shape=jax.ShapeDtypeStruct((n_slots, D_FF), BF16),
        compiler_params=_params(2),
        name="moe_swiglu_up",
    )(tile_expert, n_used, xs, w_gate_up, w_gate_up)


def _group_down(act, w_down, tile_expert, n_used):
    n_slots, k = act.shape
    n = w_down.shape[2]
    n_tiles = n_slots // TM_GROUP
    last = lambda t, nu: jnp.minimum(t, nu[0] - 1)
    grid_spec = pltpu.PrefetchScalarGridSpec(
        num_scalar_prefetch=2,
        grid=(n_tiles,),
        in_specs=[pl.BlockSpec((TM_GROUP, k), lambda t, te, nu: (last(t, nu), 0)),
                  pl.BlockSpec((1, k, n), lambda t, te, nu: (te[last(t, nu)], 0, 0))],
        out_specs=pl.BlockSpec((TM_GROUP, n), lambda t, te, nu: (t, 0)),
    )
    return pl.pallas_call(
        _group_down_body,
        grid_spec=grid_spec,
        out_shape=jax.ShapeDtypeStruct((n_slots, n), F32),
        compiler_params=_params(1),
        name="moe_down",
    )(tile_expert, n_used, act, w_down)


def _moe_combine_body(slot_ref, gate_ref, h_ref, y_hbm, out_ref, buf, sem):
    tm = TM_COMBINE
    _row_gather(slot_ref, TOP_K * tm, y_hbm,
                lambda e: buf.at[e & 1, pl.ds(lax.shift_right_logical(e, 1), 1)], sem)
    gates = gate_ref[...]
    out_ref[...] = h_ref[...] + gates[:, 0:1] * buf[0] + gates[:, 1:2] * buf[1]


def _moe_combine(h, y_slots, slots, gates):
    t, d = h.shape
    tm = TM_COMBINE
    n_tiles = t // tm
    return pl.pallas_call(
        _moe_combine_body,
        grid=(n_tiles,),
        in_specs=[pl.BlockSpec((1, 1, TOP_K * tm), lambda i: (i, 0, 0), memory_space=pltpu.SMEM),
                  pl.BlockSpec((tm, LANES), lambda i: (i, 0)),
                  pl.BlockSpec((tm, d), lambda i: (i, 0)),
                  pl.BlockSpec(memory_space=pl.ANY)],
        out_specs=pl.BlockSpec((tm, d), lambda i: (i, 0)),
        out_shape=jax.ShapeDtypeStruct((t, d), F32),
        scratch_shapes=[pltpu.VMEM((TOP_K, tm, d), F32), pltpu.SemaphoreType.DMA(())],
        compiler_params=_params(1),
        name="moe_combine",
    )(slots.reshape(n_tiles, 1, TOP_K * tm), gates, h, y_slots)


def _moe_layer(h, g, w_router, w_gate_up, w_down):
    t = h.shape[0]
    n_slots = t * TOP_K + N_EXPERTS * TM_GROUP
    hn, gates, idx = _norm_router(h, g, w_router)
    src_tok, slots, tile_expert, n_used = _route_layout(idx, n_slots)
    xs = _gather_rows(hn, src_tok)
    act = _group_swiglu_up(xs, w_gate_up, tile_expert, n_used)
    y_slots = _group_down(act, w_down, tile_expert, n_used)
    return _moe_combine(h, y_slots, slots, gates)


def kernel(x, mix_norm, ffn_norm, dil_w_in, dil_w_out, sb_w_qkv, sb_w_out, ffn_w_gate_up, ffn_w_down,
           moe_w_router, moe_w_gate_up, moe_w_down, final_norm):
    batch, seq, d = x.shape
    depth = mix_norm.shape[0]
    h = x.reshape(batch * seq, d)
    for i in range(depth):
        j = i // 2
        if i % 2 == 0:
            h = _dilated_mixer(h, mix_norm[i], dil_w_in[j], dil_w_out[j], batch, seq)
            hn = _rmsnorm(h, ffn_norm[i], BF16)
            act = _swiglu_up(hn, ffn_w_gate_up[j].astype(BF16))
            h = _matmul_residual(act, ffn_w_down[j].astype(BF16), h, "ffn_down")
        else:
            hn = _rmsnorm(h, mix_norm[i], BF16)
            qkv = _matmul(hn, sb_w_qkv[j].astype(BF16), BF16, "sb_qkv_proj")
            y = _stick_breaking_attention(qkv, batch, seq)
            h = _matmul_residual(y, sb_w_out[j].astype(BF16), h, "sb_out_proj")
            h = _moe_layer(h, ffn_norm[i], moe_w_router[j], moe_w_gate_up[j].astype(BF16),
                           moe_w_down[j].astype(BF16))
    return _rmsnorm(h, final_norm, F32).reshape(batch, seq, d)
```

```python
import functools

import numpy as np
import jax
import jax.numpy as jnp
from jax import lax
from jax.experimental import pallas as pl
from jax.experimental.pallas import tpu as pltpu

D_MODEL = 1024
N_HEADS = 16
HEAD_DIM = 64
D_ATTN = N_HEADS * HEAD_DIM
DILATED_GROUPS = ((128, 1), (512, 4), (2048, 16))
N_GROUPS = len(DILATED_GROUPS)
BLOCK = 128
D_FF = 3584
N_EXPERTS = 8
TOP_K = 2
RMS_EPS = 1e-6
ATTN_SCALE = 0.125
NEG_LOG2E = -float(np.float32(np.log2(np.e)))

LANES = 128
VMEM_LIMIT_BYTES = 56 * 1024 * 1024

TM = 512
TN = 1024
TN_FF = 512
TM_GROUP = 512
TM_COMBINE = 256
TM_EPILOGUE = 256
SB_CHUNK = 256
SB_HALVES = 2
SB_DEAD_CARRY = -104.0
DIL_PAIRS_PER_STAGE = 4
ROW_DMA_UNROLL = 8

F32 = jnp.float32
BF16 = jnp.bfloat16

ALIBI_SLOPES = tuple(float(np.float32(2.0 ** (-8.0 * (h + 1) / N_HEADS))) for h in range(N_HEADS))


def _params(n_axes):
    return pltpu.CompilerParams(dimension_semantics=("arbitrary",) * n_axes, vmem_limit_bytes=VMEM_LIMIT_BYTES)


def _dot(a, b):
    return jnp.dot(a, b, preferred_element_type=F32)


def _dot_nt(a, b):
    return lax.dot_general(a, b, (((1,), (1,)), ((), ())), preferred_element_type=F32)


def _split_bf16(x):
    hi = x.astype(BF16)
    lo = (x - hi.astype(F32)).astype(BF16)
    return hi, lo


def _rmsnorm_rows(x, g):
    return x * lax.rsqrt(jnp.mean(x * x, axis=-1, keepdims=True) + RMS_EPS) * g


def _rmsnorm_body(x_ref, g_ref, o_ref):
    o_ref[...] = _rmsnorm_rows(x_ref[...], g_ref[...]).astype(o_ref.dtype)


def _rmsnorm(x, g, out_dtype):
    t, d = x.shape
    return pl.pallas_call(
        _rmsnorm_body,
        grid=(t // TM,),
        in_specs=[pl.BlockSpec((TM, d), lambda i: (i, 0)), pl.BlockSpec((1, d), lambda i: (0, 0))],
        out_specs=pl.BlockSpec((TM, d), lambda i: (i, 0)),
        out_shape=jax.ShapeDtypeStruct((t, d), out_dtype),
        compiler_params=_params(1),
        name="rmsnorm",
    )(x, g.reshape(1, d))


def _rmsnorm_residue_body(x_ref, g_ref, *refs):
    o_refs, lanes_ref = refs[:-1], refs[-1]
    y = _rmsnorm_rows(x_ref[0], g_ref[...])
    n_chunks = y.shape[1] // LANES
    for c in range(n_chunks):
        lanes_ref[c] = y[:, c * LANES:(c + 1) * LANES]
    for o_ref, (_, dilation) in zip(o_refs, DILATED_GROUPS):
        if dilation == 1:
            o_ref[0, 0] = y.astype(o_ref.dtype)
            continue
        rows = TM // dilation
        for r in range(dilation):
            for c in range(n_chunks):
                picked = lanes_ref[c, pl.ds(r, rows, stride=dilation), :]
                o_ref[0, r, :, c * LANES:(c + 1) * LANES] = picked.astype(o_ref.dtype)


def _rmsnorm_by_residue(x, g, batch, seq):
    d_model = x.shape[1]
    tiles = seq // TM
    out_specs, out_shape = [], []
    for _, dilation in DILATED_GROUPS:
        out_specs.append(pl.BlockSpec((1, dilation, TM // dilation, d_model), lambda b, i: (b, 0, i, 0)))
        out_shape.append(jax.ShapeDtypeStruct((batch, dilation, seq // dilation, d_model), BF16))
    outs = pl.pallas_call(
        _rmsnorm_residue_body,
        grid=(batch, tiles),
        in_specs=[pl.BlockSpec((1, TM, d_model), lambda b, i: (b, i, 0)),
                  pl.BlockSpec((1, d_model), lambda b, i: (0, 0))],
        out_specs=out_specs,
        out_shape=out_shape,
        scratch_shapes=[pltpu.VMEM((d_model // LANES, TM, LANES), F32)],
        compiler_params=_params(2),
        name="rmsnorm_by_residue",
    )(x.reshape(batch, seq, d_model), g.reshape(1, d_model))
    return [o.reshape(batch * seq, d_model) for o in outs]


def _mm_body(x_ref, w_ref, o_ref):
    x = x_ref[...]
    for c in range(0, o_ref.shape[1], TN):
        o_ref[:, c:c + TN] = _dot(x, w_ref[:, c:c + TN]).astype(o_ref.dtype)


def _mm_res_body(x_ref, w_ref, r_ref, o_ref):
    o_ref[...] = r_ref[...] + _dot(x_ref[...], w_ref[...])


def _mm_res_norm_body(x_ref, w_ref, r_ref, g_ref, o_ref, hn_ref):
    h_new = r_ref[...] + _dot(x_ref[...], w_ref[...])
    o_ref[...] = h_new
    hn_ref[...] = _rmsnorm_rows(h_new, g_ref[...]).astype(hn_ref.dtype)


def _matmul(x, w, out_dtype, name, first_col=0, n_cols=None):
    m, k = x.shape
    n = w.shape[1] if n_cols is None else n_cols
    panel = first_col // n
    return pl.pallas_call(
        _mm_body,
        grid=(m // TM,),
        in_specs=[pl.BlockSpec((TM, k), lambda i: (i, 0)), pl.BlockSpec((k, n), lambda i: (0, panel))],
        out_specs=pl.BlockSpec((TM, n), lambda i: (i, 0)),
        out_shape=jax.ShapeDtypeStruct((m, n), out_dtype),
        compiler_params=_params(1),
        name=name,
    )(x, w)


def _matmul_residual(x, w, res, name, g_next=None):
    m, k = x.shape
    n = w.shape[1]
    row = pl.BlockSpec((TM, n), lambda i: (i, 0))
    in_specs = [pl.BlockSpec((TM, k), lambda i: (i, 0)), pl.BlockSpec((k, n), lambda i: (0, 0)), row]
    if g_next is None:
        return pl.pallas_call(
            _mm_res_body,
            grid=(m // TM,),
            in_specs=in_specs,
            out_specs=row,
            out_shape=jax.ShapeDtypeStruct((m, n), F32),
            compiler_params=_params(1),
            name=name,
        )(x, w, res)
    return pl.pallas_call(
        _mm_res_norm_body,
        grid=(m // TM,),
        in_specs=in_specs + [pl.BlockSpec((1, n), lambda i: (0, 0))],
        out_specs=[row, row],
        out_shape=[jax.ShapeDtypeStruct((m, n), F32), jax.ShapeDtypeStruct((m, n), BF16)],
        compiler_params=_params(1),
        name=name,
    )(x, w, res, g_next.reshape(1, n))


def _swiglu(g, u):
    return g * (1.0 / (1.0 + jnp.exp(-g))) * u


def _swiglu_cols(x, w_gate_up, o_ref):
    for c in range(0, D_FF, TN_FF):
        gate = _dot(x, w_gate_up[:, c:c + TN_FF])
        up = _dot(x, w_gate_up[:, D_FF + c:D_FF + c + TN_FF])
        o_ref[:, c:c + TN_FF] = _swiglu(gate, up).astype(o_ref.dtype)


def _swiglu_body(x_ref, w_ref, o_ref):
    _swiglu_cols(x_ref[...], w_ref, o_ref)


def _swiglu_up(x, w_gate_up):
    m, k = x.shape
    return pl.pallas_call(
        _swiglu_body,
        grid=(m // TM,),
        in_specs=[pl.BlockSpec((TM, k), lambda i: (i, 0)),
                  pl.BlockSpec((k, 2 * D_FF), lambda i: (0, 0))],
        out_specs=pl.BlockSpec((TM, D_FF), lambda i: (i, 0)),
        out_shape=jax.ShapeDtypeStruct((m, D_FF), BF16),
        compiler_params=_params(1),
        name="swiglu_up",
    )(x, w_gate_up)


def _alibi_window_bias(dilation):
    a = np.arange(BLOCK)[:, None]
    m = np.arange(2 * BLOCK)[None, :]
    rel = BLOCK + a - m
    in_window = (rel >= 0) & (rel <= BLOCK)
    slopes = np.asarray(ALIBI_SLOPES, np.float32)[:, None, None]
    bias = -slopes * (rel * dilation).astype(np.float32)[None]
    later = np.where(in_window[None], bias, -np.inf)
    first = np.where((m >= BLOCK)[None], later, -np.inf)
    return np.stack([first, later]).astype(np.float32)


def _dil_attn_body(q_ref, kc_ref, kp_ref, vc_ref, vp_ref, bias_ref, o_ref, lse_ref):
    n_keys = 2 * BLOCK
    lane = lax.broadcasted_iota(jnp.int32, (BLOCK, LANES), 1)
    first_head = lane < HEAD_DIM
    first_head_keys = lax.broadcasted_iota(jnp.int32, (n_keys, LANES), 1) < HEAD_DIM
    lse_tile = jnp.zeros((BLOCK, LANES), F32)

    def by_head(pair_rows):
        zero = jnp.zeros_like(pair_rows)
        return jnp.concatenate([jnp.where(first_head_keys, pair_rows, zero),
                                jnp.where(first_head_keys, zero, pair_rows)], axis=0)

    for first_pair in range(0, N_HEADS // 2, DIL_PAIRS_PER_STAGE):
        pairs = range(first_pair, first_pair + DIL_PAIRS_PER_STAGE)
        cols = {p: slice(p * LANES, (p + 1) * LANES) for p in pairs}
        scores = {p: _dot_nt(q_ref[:, cols[p]] * ATTN_SCALE,
                             by_head(jnp.concatenate([kp_ref[:, cols[p]], kc_ref[:, cols[p]]], axis=0)))
                  for p in pairs}
        probs, norms = {}, {}
        for p in pairs:
            pair_probs = []
            for hh in range(2):
                h = 2 * p + hh
                s = scores[p][:, hh * n_keys:(hh + 1) * n_keys] + bias_ref[0, h]
                s_max = jnp.max(s, axis=1, keepdims=True)
                e = jnp.exp(s - s_max)
                z = jnp.sum(e, axis=1, keepdims=True)
                lse_tile = jnp.where(lane == h, s_max + jnp.log(z), lse_tile)
                pair_probs.append(e.astype(BF16))
                norms[h] = z
            probs[p] = jnp.concatenate(pair_probs, axis=1)
        for p in pairs:
            v_heads = by_head(jnp.concatenate([vp_ref[:, cols[p]], vc_ref[:, cols[p]]], axis=0))
            o_pair = _dot(probs[p], v_heads) / jnp.where(first_head, norms[2 * p], norms[2 * p + 1])
            o_ref[0, :, cols[p]] = o_pair.astype(o_ref.dtype)
    lse_ref[0] = lse_tile


def _dilated_attention(proj, batch, seq, group):
    _, dilation = DILATED_GROUPS[group]
    sub_len = seq // dilation
    n_blocks = sub_len // BLOCK

    def block(which, back):
        return lambda b, r, n: ((b * dilation + r) * n_blocks + jnp.maximum(n - back, 0), which)

    blk = (BLOCK, D_ATTN)
    bias = jnp.asarray(_alibi_window_bias(dilation))
    out, lse = pl.pallas_call(
        _dil_attn_body,
        grid=(batch, dilation, n_blocks),
        in_specs=[pl.BlockSpec(blk, block(0, 0)), pl.BlockSpec(blk, block(1, 0)), pl.BlockSpec(blk, block(1, 1)),
                  pl.BlockSpec(blk, block(2, 0)), pl.BlockSpec(blk, block(2, 1)),
                  pl.BlockSpec((1,) + bias.shape[1:], lambda b, r, n: (jnp.minimum(n, 1), 0, 0, 0))],
        out_specs=[pl.BlockSpec((1, BLOCK, D_ATTN), lambda b, r, n: (b, n, r)),
                   pl.BlockSpec((1, BLOCK, LANES), lambda b, r, n: (b, n, r))],
        out_shape=[jax.ShapeDtypeStruct((batch, sub_len, dilation * D_ATTN), BF16),
                   jax.ShapeDtypeStruct((batch, sub_len, dilation * LANES), F32)],
        compiler_params=_params(3),
        name=f"dilated_attn_g{group}",
    )(proj, proj, proj, proj, proj, bias)
    return out.reshape(batch * seq, D_ATTN), lse.reshape(batch * seq, LANES)


def _dil_out_body(o0_ref, o1_ref, o2_ref, l0_ref, l1_ref, l2_ref, e_ref, w_ref, h_ref, g_ref, out_ref, hn_ref):
    lses = [l0_ref[...], l1_ref[...], l2_ref[...]]
    outs = [o0_ref, o1_ref, o2_ref]
    top = jnp.maximum(jnp.maximum(lses[0], lses[1]), lses[2])
    es = [jnp.exp(l - top) for l in lses]
    den = es[0] + es[1] + es[2]
    expand = e_ref[...]
    y = None
    for g in range(N_GROUPS):
        hi, lo = _split_bf16(es[g] / den)
        term = (_dot(hi, expand) + _dot(lo, expand)) * outs[g][...].astype(F32)
        y = term if y is None else y + term
    h_new = h_ref[...] + _dot(y.astype(BF16), w_ref[...])
    out_ref[...] = h_new
    hn_ref[...] = _rmsnorm_rows(h_new, g_ref[...]).astype(hn_ref.dtype)


def _dilated_out_proj(outs, lses, w_out, h, g_next):
    t = h.shape[0]
    tm = TM_EPILOGUE
    expand = (np.arange(LANES)[:, None] == (np.arange(D_ATTN)[None, :] // HEAD_DIM)).astype(np.float32)
    row = lambda w: pl.BlockSpec((tm, w), lambda i: (i, 0))
    full = lambda r, c: pl.BlockSpec((r, c), lambda i: (0, 0))
    return pl.pallas_call(
        _dil_out_body,
        grid=(t // tm,),
        in_specs=[row(D_ATTN)] * 3 + [row(LANES)] * 3 + [full(LANES, D_ATTN), full(D_ATTN, D_MODEL), row(D_MODEL),
                                                          full(1, D_MODEL)],
        out_specs=[row(D_MODEL), row(D_MODEL)],
        out_shape=[jax.ShapeDtypeStruct((t, D_MODEL), F32), jax.ShapeDtypeStruct((t, D_MODEL), BF16)],
        compiler_params=_params(1),
        name="dilated_out_proj",
    )(*outs, *lses, jnp.asarray(expand, BF16), w_out, h, g_next.reshape(1, D_MODEL))


def _dilated_mixer(h, g, w_in, w_out, g_next, batch, seq):
    w_in = w_in.astype(BF16)
    parts = []
    for group, hn in enumerate(_rmsnorm_by_residue(h, g, batch, seq)):
        proj = _matmul(hn, w_in, BF16, f"dilated_in_proj_g{group}", first_col=group * 3 * D_ATTN, n_cols=3 * D_ATTN)
        parts.append(_dilated_attention(proj, batch, seq, group))
    return _dilated_out_proj([p[0] for p in parts], [p[1] for p in parts], w_out.astype(BF16), h, g_next)


def _sb_attn_body(q_ref, k_ref, v_ref, tri_ref, o_ref):
    i = pl.program_id(2)
    row = lax.broadcasted_iota(jnp.int32, (SB_CHUNK, SB_CHUNK), 0)
    col = lax.broadcasted_iota(jnp.int32, (SB_CHUNK, SB_CHUNK), 1)
    strict = col < row
    first_head = lax.broadcasted_iota(jnp.int32, (SB_CHUNK, LANES), 1) < HEAD_DIM
    head_lanes = (first_head, ~first_head)
    tri = tri_ref[...]
    heads = range(2)
    q_heads = []
    for half in range(SB_HALVES):
        q_pair = q_ref[0, half * SB_CHUNK:(half + 1) * SB_CHUNK, :] * ATTN_SCALE
        q_heads.append([jnp.where(sel, q_pair, jnp.zeros_like(q_pair)) for sel in head_lanes])

    def chunk(j, state, work):
        start = pl.multiple_of(j * SB_CHUNK, SB_CHUNK)
        k_j = k_ref[0, pl.ds(start, SB_CHUNK), :]
        v_j = v_ref[0, pl.ds(start, SB_CHUNK), :]
        v_heads = [jnp.where(sel, v_j, jnp.zeros_like(v_j)) for sel in head_lanes]
        chains = [(half, hh, diag) for half, diag in work for hh in heads]
        zs = [_dot_nt(q_heads[half][hh], k_j) for half, hh, _ in chains]
        zs = [jnp.where(strict, z, -1e30) if diag else z for z, (_, _, diag) in zip(zs, chains)]
        sps = [jnp.maximum(z, 0.0) + jnp.log(1.0 + jnp.exp2(jnp.abs(z) * NEG_LOG2E)) for z in zs]
        later = [_dot(sp.astype(BF16), tri) for sp in sps]
        atts = [jnp.exp((z - sp) + after + state[half][0][hh]).astype(BF16)
                for z, sp, after, (half, hh, _) in zip(zs, sps, later, chains)]
        new_state = list(state)
        for c, (half, hh, _) in enumerate(chains):
            carries, acc = new_state[half]
            chunk_sum = later[c][:, 0:1] - sps[c][:, 0:1]
            carries = tuple(carries[x] + chunk_sum if x == hh else carries[x] for x in heads)
            new_state[half] = (carries, acc + _dot(atts[c], v_heads[hh]))
        return tuple(new_state)

    def max_carry(state):
        return functools.reduce(jnp.maximum, [jnp.max(c) for carries, _ in state for c in carries])

    zero_carry = jnp.zeros((SB_CHUNK, 1), F32)
    state = (((zero_carry, zero_carry), jnp.zeros((SB_CHUNK, LANES), F32)),) * SB_HALVES
    state = chunk(SB_HALVES * i + 1, state, ((1, True),))
    state = chunk(SB_HALVES * i, state, ((0, True), (1, False)))

    def live(loop):
        jj, _, carry_bound = loop
        return jnp.logical_and(jj < SB_HALVES * i, carry_bound > SB_DEAD_CARRY)

    def step(loop):
        jj, state, _ = loop
        state = chunk(SB_HALVES * i - 1 - jj, state, ((0, False), (1, False)))
        return jj + 1, state, max_carry(state)

    _, state, _ = lax.while_loop(live, step, (jnp.int32(0), state, max_carry(state)))
    for half in range(SB_HALVES):
        o_ref[0, half * SB_CHUNK:(half + 1) * SB_CHUNK, :] = state[half][1].astype(o_ref.dtype)


def _stick_breaking_attention(qkv, batch, seq):
    n_pairs = N_HEADS // 2
    view = qkv.reshape(batch, seq, 3 * D_ATTN)
    q_rows = SB_HALVES * SB_CHUNK
    j = np.arange(SB_CHUNK)
    tri = -(j[:, None] > j[None, :]).astype(np.float32)
    out = pl.pallas_call(
        _sb_attn_body,
        grid=(batch, n_pairs, seq // q_rows),
        in_specs=[pl.BlockSpec((1, q_rows, LANES), lambda b, p, i: (b, i, p)),
                  pl.BlockSpec((1, seq, LANES), lambda b, p, i: (b, 0, n_pairs + p)),
                  pl.BlockSpec((1, seq, LANES), lambda b, p, i: (b, 0, 2 * n_pairs + p)),
                  pl.BlockSpec((SB_CHUNK, SB_CHUNK), lambda b, p, i: (0, 0))],
        out_specs=pl.BlockSpec((1, q_rows, LANES), lambda b, p, i: (b, i, p)),
        out_shape=jax.ShapeDtypeStruct((batch, seq, D_ATTN), BF16),
        compiler_params=_params(3),
        name="stick_breaking_attn",
    )(view, view, view, jnp.asarray(tri, BF16))
    return out.reshape(batch * seq, D_ATTN)


def _norm_router_body(h_ref, g_ref, wh_ref, wl_ref, hn_ref, gate_ref, idx_ref):
    y = _rmsnorm_rows(h_ref[...], g_ref[...])
    hn_ref[...] = y
    y_hi, y_lo = _split_bf16(y)
    w_hi, w_lo = wh_ref[...], wl_ref[...]
    logits = _dot(y_hi, w_hi) + _dot(y_hi, w_lo) + _dot(y_lo, w_hi)
    lane = lax.broadcasted_iota(jnp.int32, logits.shape, 1)
    neg_inf = F32(-jnp.inf)
    logits = jnp.where(lane < N_EXPERTS, logits, neg_inf)
    v1 = jnp.max(logits, axis=1, keepdims=True)
    i1 = jnp.min(jnp.where(logits == v1, lane, LANES), axis=1, keepdims=True)
    rest = jnp.where(lane == i1, neg_inf, logits)
    v2 = jnp.max(rest, axis=1, keepdims=True)
    i2 = jnp.min(jnp.where(rest == v2, lane, LANES), axis=1, keepdims=True)
    e2 = jnp.exp(v2 - v1)
    w1 = 1.0 / (1.0 + e2)
    w2 = e2 * w1
    gate_ref[...] = jnp.where(lane == 0, w1, jnp.where(lane == 1, w2, 0.0))
    idx_ref[...] = jnp.where(lane == 0, i1, jnp.where(lane == 1, i2, 0))


def _norm_router(h, g, w_router):
    t, d = h.shape
    tm = TM_EPILOGUE
    w_pad = jnp.zeros((d, LANES), F32).at[:, :N_EXPERTS].set(w_router)
    w_hi = w_pad.astype(BF16)
    w_lo = (w_pad - w_hi.astype(F32)).astype(BF16)
    row = lambda w: pl.BlockSpec((tm, w), lambda i: (i, 0))
    full = lambda r, c: pl.BlockSpec((r, c), lambda i: (0, 0))
    return pl.pallas_call(
        _norm_router_body,
        grid=(t // tm,),
        in_specs=[row(d), full(1, d), full(d, LANES), full(d, LANES)],
        out_specs=[row(d), row(LANES), row(LANES)],
        out_shape=[jax.ShapeDtypeStruct((t, d), F32), jax.ShapeDtypeStruct((t, LANES), F32),
                   jax.ShapeDtypeStruct((t, LANES), jnp.int32)],
        compiler_params=_params(1),
        name="norm_router",
    )(h, g.reshape(1, d), w_hi, w_lo)


def _route_layout(idx, n_slots):
    n_tiles = n_slots // TM_GROUP
    flat = idx[:, :TOP_K].reshape(-1)
    onehot = (flat[:, None] == jnp.arange(N_EXPERTS, dtype=jnp.int32)[None, :]).astype(jnp.int32)
    csum = jnp.cumsum(onehot, axis=0)
    rank = jnp.sum(csum * onehot, axis=1) - 1
    counts = csum[-1]
    padded = ((counts + TM_GROUP - 1) // TM_GROUP) * TM_GROUP
    pad_end = jnp.cumsum(padded)
    dest = (pad_end - padded)[flat] + rank
    tokens = jnp.arange(flat.shape[0], dtype=jnp.int32) // TOP_K
    src_tok = jnp.zeros((n_slots,), jnp.int32).at[dest].set(tokens)
    tile_start = jnp.arange(n_tiles, dtype=jnp.int32) * TM_GROUP
    tile_expert = jnp.minimum(jnp.sum((tile_start[:, None] >= pad_end[None, :]).astype(jnp.int32), axis=1),
                              N_EXPERTS - 1)
    n_used = (pad_end[-1:] // TM_GROUP).astype(jnp.int32)
    return src_tok, dest.reshape(-1, TOP_K), tile_expert, n_used


def _row_copies(wait, idx_ref, n_rows, src_hbm, dst_for_row, sem):
    def copy(r):
        return pltpu.make_async_copy(src_hbm.at[pl.ds(idx_ref[0, 0, r], 1)], dst_for_row(r), sem)

    def trip(g, c):
        for u in range(ROW_DMA_UNROLL):
            if wait:
                copy(g * ROW_DMA_UNROLL + u).wait()
            else:
                copy(g * ROW_DMA_UNROLL + u).start(priority=u % 2)
        return c

    lax.fori_loop(0, n_rows // ROW_DMA_UNROLL, trip, 0)


def _prefetched_row_gather(idx_ref, next_idx_ref, n_rows, src_hbm, dst_for_row, sems):
    i = pl.program_id(0)
    cur = i & 1

    @pl.when(i == 0)
    def _():
        _row_copies(False, idx_ref, n_rows, src_hbm, functools.partial(dst_for_row, 0), sems.at[0])

    @pl.when(i + 1 < pl.num_programs(0))
    def _():
        _row_copies(False, next_idx_ref, n_rows, src_hbm, functools.partial(dst_for_row, 1 - cur), sems.at[1 - cur])

    _row_copies(True, idx_ref, n_rows, src_hbm, functools.partial(dst_for_row, cur), sems.at[cur])
    return cur


def _gather_body(src_ref, next_src_ref, x_hbm, o_ref, buf, sems):
    cur = _prefetched_row_gather(src_ref, next_src_ref, TM_GROUP, x_hbm,
                                 lambda b, r: buf.at[b, pl.ds(r, 1)], sems)
    o_ref[...] = buf[cur].astype(o_ref.dtype)


def _gather_rows(x, src_tok):
    d = x.shape[1]
    n_slots = src_tok.shape[0]
    n_tiles = n_slots // TM_GROUP
    src_idx = src_tok.reshape(n_tiles, 1, TM_GROUP)
    return pl.pallas_call(
        _gather_body,
        grid=(n_tiles,),
        in_specs=[pl.BlockSpec((1, 1, TM_GROUP), lambda i: (i, 0, 0), memory_space=pltpu.SMEM),
                  pl.BlockSpec((1, 1, TM_GROUP), lambda i: (jnp.minimum(i + 1, n_tiles - 1), 0, 0),
                               memory_space=pltpu.SMEM),
                  pl.BlockSpec(memory_space=pl.ANY)],
        out_specs=pl.BlockSpec((TM_GROUP, d), lambda i: (i, 0)),
        out_shape=jax.ShapeDtypeStruct((n_slots, d), BF16),
        scratch_shapes=[pltpu.VMEM((2, TM_GROUP, d), F32), pltpu.SemaphoreType.DMA((2,))],
        compiler_params=_params(1),
        name="moe_gather",
    )(src_idx, src_idx, x)


def _group_swiglu_body(te_ref, nu_ref, x_ref, w_ref, o_ref):
    used = pl.program_id(0) < nu_ref[0]

    @pl.when(used)
    def _():
        _swiglu_cols(x_ref[...], w_ref.at[0], o_ref)

    @pl.when(jnp.logical_not(used))
    def _():
        o_ref[...] = jnp.zeros_like(o_ref)


def _group_down_body(te_ref, nu_ref, x_ref, w_ref, o_ref):
    used = pl.program_id(0) < nu_ref[0]

    @pl.when(used)
    def _():
        o_ref[...] = _dot(x_ref[...], w_ref[0])

    @pl.when(jnp.logical_not(used))
    def _():
        o_ref[...] = jnp.zeros_like(o_ref)


def _group_swiglu_up(xs, w_gate_up, tile_expert, n_used):
    n_slots, k = xs.shape
    n_tiles = n_slots // TM_GROUP
    last = lambda t, nu: jnp.minimum(t, nu[0] - 1)
    grid_spec = pltpu.PrefetchScalarGridSpec(
        num_scalar_prefetch=2,
        grid=(n_tiles,),
        in_specs=[pl.BlockSpec((TM_GROUP, k), lambda t, te, nu: (last(t, nu), 0)),
                  pl.BlockSpec((1, k, 2 * D_FF), lambda t, te, nu: (te[last(t, nu)], 0, 0))],
        out_specs=pl.BlockSpec((TM_GROUP, D_FF), lambda t, te, nu: (t, 0)),
    )
    return pl.pallas_call(
        _group_swiglu_body,
        grid_spec=grid_spec,
        out_shape=jax.ShapeDtypeStruct((n_slots, D_FF), BF16),
        compiler_params=_params(1),
        name="moe_swiglu_up",
    )(tile_expert, n_used, xs, w_gate_up)


def _group_down(act, w_down, tile_expert, n_used):
    n_slots, k = act.shape
    n = w_down.shape[2]
    n_tiles = n_slots // TM_GROUP
    last = lambda t, nu: jnp.minimum(t, nu[0] - 1)
    grid_spec = pltpu.PrefetchScalarGridSpec(
        num_scalar_prefetch=2,
        grid=(n_tiles,),
        in_specs=[pl.BlockSpec((TM_GROUP, k), lambda t, te, nu: (last(t, nu), 0)),
                  pl.BlockSpec((1, k, n), lambda t, te, nu: (te[last(t, nu)], 0, 0))],
        out_specs=pl.BlockSpec((TM_GROUP, n), lambda t, te, nu: (t, 0)),
    )
    return pl.pallas_call(
        _group_down_body,
        grid_spec=grid_spec,
        out_shape=jax.ShapeDtypeStruct((n_slots, n), F32),
        compiler_params=_params(1),
        name="moe_down",
    )(tile_expert, n_used, act, w_down)


def _combined_rows(slot_ref, next_slot_ref, gate_ref, h_ref, y_hbm, buf, sems):
    cur = _prefetched_row_gather(
        slot_ref, next_slot_ref, TOP_K * TM_COMBINE, y_hbm,
        lambda b, e: buf.at[b, e & 1, pl.ds(lax.shift_right_logical(e, 1), 1)], sems)
    gates = gate_ref[...]
    return h_ref[...] + gates[:, 0:1] * buf[cur, 0] + gates[:, 1:2] * buf[cur, 1]


def _moe_combine_body(slot_ref, next_slot_ref, gate_ref, h_ref, y_hbm, out_ref, buf, sems):
    out_ref[...] = _combined_rows(slot_ref, next_slot_ref, gate_ref, h_ref, y_hbm, buf, sems)


def _moe_combine_norm_body(slot_ref, next_slot_ref, gate_ref, h_ref, g_ref, y_hbm, out_ref, buf, sems):
    rows = _combined_rows(slot_ref, next_slot_ref, gate_ref, h_ref, y_hbm, buf, sems)
    out_ref[...] = _rmsnorm_rows(rows, g_ref[...])


def _moe_combine(h, y_slots, slots, gates, g_final=None):
    t, d = h.shape
    tm = TM_COMBINE
    n_tiles = t // tm
    slot_idx = slots.reshape(n_tiles, 1, TOP_K * tm)
    in_specs = [pl.BlockSpec((1, 1, TOP_K * tm), lambda i: (i, 0, 0), memory_space=pltpu.SMEM),
                pl.BlockSpec((1, 1, TOP_K * tm), lambda i: (jnp.minimum(i + 1, n_tiles - 1), 0, 0),
                             memory_space=pltpu.SMEM),
                pl.BlockSpec((tm, LANES), lambda i: (i, 0)),
                pl.BlockSpec((tm, d), lambda i: (i, 0))]
    args = [slot_idx, slot_idx, gates, h]
    if g_final is not None:
        in_specs.append(pl.BlockSpec((1, d), lambda i: (0, 0)))
        args.append(g_final.reshape(1, d))
    return pl.pallas_call(
        _moe_combine_body if g_final is None else _moe_combine_norm_body,
        grid=(n_tiles,),
        in_specs=in_specs + [pl.BlockSpec(memory_space=pl.ANY)],
        out_specs=pl.BlockSpec((tm, d), lambda i: (i, 0)),
        out_shape=jax.ShapeDtypeStruct((t, d), F32),
        scratch_shapes=[pltpu.VMEM((2, TOP_K, tm, d), F32), pltpu.SemaphoreType.DMA((2,))],
        compiler_params=_params(1),
        name="moe_combine",
    )(*args, y_slots)


def _moe_layer(h, g, w_router, w_gate_up, w_down, g_final=None):
    t = h.shape[0]
    n_slots = t * TOP_K + N_EXPERTS * TM_GROUP
    hn, gates, idx = _norm_router(h, g, w_router)
    src_tok, slots, tile_expert, n_used = _route_layout(idx, n_slots)
    xs = _gather_rows(hn, src_tok)
    act = _group_swiglu_up(xs, w_gate_up, tile_expert, n_used)
    y_slots = _group_down(act, w_down, tile_expert, n_used)
    return _moe_combine(h, y_slots, slots, gates, g_final)


def kernel(x, mix_norm, ffn_norm, dil_w_in, dil_w_out, sb_w_qkv, sb_w_out, ffn_w_gate_up, ffn_w_down,
           moe_w_router, moe_w_gate_up, moe_w_down, final_norm):
    batch, seq, d = x.shape
    depth = mix_norm.shape[0]
    h = x.reshape(batch * seq, d)
    hn = None
    for i in range(depth):
        j = i // 2
        last = i + 1 == depth
        if i % 2 == 0:
            h, hn = _dilated_mixer(h, mix_norm[i], dil_w_in[j], dil_w_out[j], ffn_norm[i], batch, seq)
            act = _swiglu_up(hn, ffn_w_gate_up[j].astype(BF16))
            if last:
                h, hn = _matmul_residual(act, ffn_w_down[j].astype(BF16), h, "ffn_down"), None
            else:
                h, hn = _matmul_residual(act, ffn_w_down[j].astype(BF16), h, "ffn_down", g_next=mix_norm[i + 1])
        else:
            if hn is None:
                hn = _rmsnorm(h, mix_norm[i], BF16)
            qkv = _matmul(hn, sb_w_qkv[j].astype(BF16), BF16, "sb_qkv_proj")
            y = _stick_breaking_attention(qkv, batch, seq)
            h = _matmul_residual(y, sb_w_out[j].astype(BF16), h, "sb_out_proj")
            h = _moe_layer(h, ffn_norm[i], moe_w_router[j], moe_w_gate_up[j].astype(BF16),
                           moe_w_down[j].astype(BF16), g_final=final_norm if last else None)
            hn = None
            if last:
                return h.reshape(batch, seq, d)
    return _rmsnorm(h, final_norm, F32).reshape(batch, seq, d)
```

```python
import functools

import numpy as np
import jax
import jax.numpy as jnp
from jax import lax
from jax.experimental import pallas as pl
from jax.experimental.pallas import tpu as pltpu

D_MODEL = 1024
N_HEADS = 16
HEAD_DIM = 64
D_ATTN = N_HEADS * HEAD_DIM
DILATED_GROUPS = ((128, 1), (512, 4), (2048, 16))
N_GROUPS = len(DILATED_GROUPS)
BLOCK = 128
D_FF = 3584
N_EXPERTS = 8
TOP_K = 2
RMS_EPS = 1e-6
ATTN_SCALE = 0.125
NEG_LOG2E = -float(np.float32(np.log2(np.e)))

LANES = 128
VMEM_LIMIT_BYTES = 56 * 1024 * 1024

TM = 512
TN = 1024
TN_FF = 512
TM_GROUP = 512
TM_COMBINE = 256
TM_EPILOGUE = 256
SB_CHUNK = 256
SB_HALVES = 2
SB_DEAD_CARRY = -104.0
DIL_PAIRS_PER_STAGE = 4
ROW_DMA_UNROLL = 8
ROW_TILE = D_MODEL // LANES

F32 = jnp.float32
BF16 = jnp.bfloat16

ALIBI_SLOPES = tuple(float(np.float32(2.0 ** (-8.0 * (h + 1) / N_HEADS))) for h in range(N_HEADS))


def _params(n_axes):
    return pltpu.CompilerParams(dimension_semantics=("arbitrary",) * n_axes, vmem_limit_bytes=VMEM_LIMIT_BYTES)


def _dot(a, b):
    return jnp.dot(a, b, preferred_element_type=F32)


def _dot_nt(a, b):
    return lax.dot_general(a, b, (((1,), (1,)), ((), ())), preferred_element_type=F32)


def _split_bf16(x):
    hi = x.astype(BF16)
    lo = (x - hi.astype(F32)).astype(BF16)
    return hi, lo


def _rmsnorm_rows(x, g):
    return x * lax.rsqrt(jnp.mean(x * x, axis=-1, keepdims=True) + RMS_EPS) * g


def _rmsnorm_body(x_ref, g_ref, o_ref):
    o_ref[...] = _rmsnorm_rows(x_ref[...], g_ref[...]).astype(o_ref.dtype)


def _rmsnorm(x, g, out_dtype):
    t, d = x.shape
    return pl.pallas_call(
        _rmsnorm_body,
        grid=(t // TM,),
        in_specs=[pl.BlockSpec((TM, d), lambda i: (i, 0)), pl.BlockSpec((1, d), lambda i: (0, 0))],
        out_specs=pl.BlockSpec((TM, d), lambda i: (i, 0)),
        out_shape=jax.ShapeDtypeStruct((t, d), out_dtype),
        compiler_params=_params(1),
        name="rmsnorm",
    )(x, g.reshape(1, d))


def _rmsnorm_residue_body(x_ref, g_ref, *refs):
    o_refs, lanes_ref = refs[:-1], refs[-1]
    y = _rmsnorm_rows(x_ref[0], g_ref[...])
    n_chunks = y.shape[1] // LANES
    for c in range(n_chunks):
        lanes_ref[c] = y[:, c * LANES:(c + 1) * LANES]
    for o_ref, (_, dilation) in zip(o_refs, DILATED_GROUPS):
        if dilation == 1:
            o_ref[0, 0] = y.astype(o_ref.dtype)
            continue
        rows = TM // dilation
        for r in range(dilation):
            for c in range(n_chunks):
                picked = lanes_ref[c, pl.ds(r, rows, stride=dilation), :]
                o_ref[0, r, :, c * LANES:(c + 1) * LANES] = picked.astype(o_ref.dtype)


def _rmsnorm_by_residue(x, g, batch, seq):
    d_model = x.shape[1]
    tiles = seq // TM
    out_specs, out_shape = [], []
    for _, dilation in DILATED_GROUPS:
        out_specs.append(pl.BlockSpec((1, dilation, TM // dilation, d_model), lambda b, i: (b, 0, i, 0)))
        out_shape.append(jax.ShapeDtypeStruct((batch, dilation, seq // dilation, d_model), BF16))
    outs = pl.pallas_call(
        _rmsnorm_residue_body,
        grid=(batch, tiles),
        in_specs=[pl.BlockSpec((1, TM, d_model), lambda b, i: (b, i, 0)),
                  pl.BlockSpec((1, d_model), lambda b, i: (0, 0))],
        out_specs=out_specs,
        out_shape=out_shape,
        scratch_shapes=[pltpu.VMEM((d_model // LANES, TM, LANES), F32)],
        compiler_params=_params(2),
        name="rmsnorm_by_residue",
    )(x.reshape(batch, seq, d_model), g.reshape(1, d_model))
    return [o.reshape(batch * seq, d_model) for o in outs]


def _mm_body(x_ref, w_ref, o_ref):
    x = x_ref[...]
    for c in range(0, o_ref.shape[1], TN):
        o_ref[:, c:c + TN] = _dot(x, w_ref[:, c:c + TN]).astype(o_ref.dtype)


def _mm_res_body(x_ref, w_ref, r_ref, o_ref):
    o_ref[...] = r_ref[...] + _dot(x_ref[...], w_ref[...])


def _mm_res_norm_body(x_ref, w_ref, r_ref, g_ref, o_ref, hn_ref):
    h_new = r_ref[...] + _dot(x_ref[...], w_ref[...])
    o_ref[...] = h_new
    hn_ref[...] = _rmsnorm_rows(h_new, g_ref[...]).astype(hn_ref.dtype)


def _matmul(x, w, out_dtype, name, first_col=0, n_cols=None):
    m, k = x.shape
    n = w.shape[1] if n_cols is None else n_cols
    panel = first_col // n
    return pl.pallas_call(
        _mm_body,
        grid=(m // TM,),
        in_specs=[pl.BlockSpec((TM, k), lambda i: (i, 0)), pl.BlockSpec((k, n), lambda i: (0, panel))],
        out_specs=pl.BlockSpec((TM, n), lambda i: (i, 0)),
        out_shape=jax.ShapeDtypeStruct((m, n), out_dtype),
        compiler_params=_params(1),
        name=name,
    )(x, w)


def _matmul_residual(x, w, res, name, g_next=None):
    m, k = x.shape
    n = w.shape[1]
    row = pl.BlockSpec((TM, n), lambda i: (i, 0))
    in_specs = [pl.BlockSpec((TM, k), lambda i: (i, 0)), pl.BlockSpec((k, n), lambda i: (0, 0)), row]
    if g_next is None:
        return pl.pallas_call(
            _mm_res_body,
            grid=(m // TM,),
            in_specs=in_specs,
            out_specs=row,
            out_shape=jax.ShapeDtypeStruct((m, n), F32),
            compiler_params=_params(1),
            name=name,
        )(x, w, res)
    return pl.pallas_call(
        _mm_res_norm_body,
        grid=(m // TM,),
        in_specs=in_specs + [pl.BlockSpec((1, n), lambda i: (0, 0))],
        out_specs=[row, row],
        out_shape=[jax.ShapeDtypeStruct((m, n), F32), jax.ShapeDtypeStruct((m, n), BF16)],
        compiler_params=_params(1),
        name=name,
    )(x, w, res, g_next.reshape(1, n))


def _swiglu(g, u):
    return g * (1.0 / (1.0 + jnp.exp(-g))) * u


def _swiglu_cols(x, w_gate_up, o_ref):
    for c in range(0, D_FF, TN_FF):
        gate = _dot(x, w_gate_up[:, c:c + TN_FF])
        up = _dot(x, w_gate_up[:, D_FF + c:D_FF + c + TN_FF])
        o_ref[:, c:c + TN_FF] = _swiglu(gate, up).astype(o_ref.dtype)


def _swiglu_body(x_ref, w_ref, o_ref):
    _swiglu_cols(x_ref[...], w_ref, o_ref)


def _swiglu_up(x, w_gate_up):
    m, k = x.shape
    return pl.pallas_call(
        _swiglu_body,
        grid=(m // TM,),
        in_specs=[pl.BlockSpec((TM, k), lambda i: (i, 0)),
                  pl.BlockSpec((k, 2 * D_FF), lambda i: (0, 0))],
        out_specs=pl.BlockSpec((TM, D_FF), lambda i: (i, 0)),
        out_shape=jax.ShapeDtypeStruct((m, D_FF), BF16),
        compiler_params=_params(1),
        name="swiglu_up",
    )(x, w_gate_up)


def _alibi_window_bias(dilation):
    a = np.arange(BLOCK)[:, None]
    m = np.arange(2 * BLOCK)[None, :]
    rel = BLOCK + a - m
    in_window = (rel >= 0) & (rel <= BLOCK)
    slopes = np.asarray(ALIBI_SLOPES, np.float32)[:, None, None]
    bias = -slopes * (rel * dilation).astype(np.float32)[None]
    later = np.where(in_window[None], bias, -np.inf)
    first = np.where((m >= BLOCK)[None], later, -np.inf)
    return np.stack([first, later]).astype(np.float32)


def _dil_attn_body(q_ref, kc_ref, kp_ref, vc_ref, vp_ref, bias_ref, o_ref, lse_ref):
    n_keys = 2 * BLOCK
    lane = lax.broadcasted_iota(jnp.int32, (BLOCK, LANES), 1)
    first_head = lane < HEAD_DIM
    first_head_keys = lax.broadcasted_iota(jnp.int32, (n_keys, LANES), 1) < HEAD_DIM
    lse_tile = jnp.zeros((BLOCK, LANES), F32)

    def by_head(pair_rows):
        zero = jnp.zeros_like(pair_rows)
        return jnp.concatenate([jnp.where(first_head_keys, pair_rows, zero),
                                jnp.where(first_head_keys, zero, pair_rows)], axis=0)

    for first_pair in range(0, N_HEADS // 2, DIL_PAIRS_PER_STAGE):
        pairs = range(first_pair, first_pair + DIL_PAIRS_PER_STAGE)
        cols = {p: slice(p * LANES, (p + 1) * LANES) for p in pairs}
        scores = {p: _dot_nt(q_ref[:, cols[p]] * ATTN_SCALE,
                             by_head(jnp.concatenate([kp_ref[:, cols[p]], kc_ref[:, cols[p]]], axis=0)))
                  for p in pairs}
        probs, norms = {}, {}
        for p in pairs:
            pair_probs = []
            for hh in range(2):
                h = 2 * p + hh
                s = scores[p][:, hh * n_keys:(hh + 1) * n_keys] + bias_ref[0, h]
                s_max = jnp.max(s, axis=1, keepdims=True)
                e = jnp.exp(s - s_max)
                z = jnp.sum(e, axis=1, keepdims=True)
                lse_tile = jnp.where(lane == h, s_max + jnp.log(z), lse_tile)
                pair_probs.append(e.astype(BF16))
                norms[h] = z
            probs[p] = jnp.concatenate(pair_probs, axis=1)
        for p in pairs:
            v_heads = by_head(jnp.concatenate([vp_ref[:, cols[p]], vc_ref[:, cols[p]]], axis=0))
            o_pair = _dot(probs[p], v_heads) / jnp.where(first_head, norms[2 * p], norms[2 * p + 1])
            o_ref[:, cols[p]] = o_pair.astype(o_ref.dtype)
    lse_ref[...] = lse_tile


def _dilated_attention(proj, batch, seq, group):
    _, dilation = DILATED_GROUPS[group]
    n_blocks = seq // dilation // BLOCK

    def block(which, back):
        return lambda b, r, n: ((b * dilation + r) * n_blocks + jnp.maximum(n - back, 0), which)

    blk = (BLOCK, D_ATTN)
    bias = jnp.asarray(_alibi_window_bias(dilation))
    return pl.pallas_call(
        _dil_attn_body,
        grid=(batch, dilation, n_blocks),
        in_specs=[pl.BlockSpec(blk, block(0, 0)), pl.BlockSpec(blk, block(1, 0)), pl.BlockSpec(blk, block(1, 1)),
                  pl.BlockSpec(blk, block(2, 0)), pl.BlockSpec(blk, block(2, 1)),
                  pl.BlockSpec((1,) + bias.shape[1:], lambda b, r, n: (jnp.minimum(n, 1), 0, 0, 0))],
        out_specs=[pl.BlockSpec(blk, block(0, 0)), pl.BlockSpec((BLOCK, LANES), block(0, 0))],
        out_shape=[jax.ShapeDtypeStruct((batch * seq, D_ATTN), BF16),
                   jax.ShapeDtypeStruct((batch * seq, LANES), F32)],
        compiler_params=_params(3),
        name=f"dilated_attn_g{group}",
    )(proj, proj, proj, proj, proj, bias)


def _dil_out_body(o0_ref, o1_ref, o2_ref, l0_ref, l1_ref, l2_ref, e_ref, w_ref, h_ref, g_ref, out_ref, hn_ref,
                  rows_ref, stat_ref):
    tile = h_ref.shape[1]
    n_chunks = D_ATTN // LANES
    outs, lses = [], []
    for grp, (o_ref, l_ref) in enumerate(zip((o0_ref, o1_ref, o2_ref), (l0_ref, l1_ref, l2_ref))):
        dilation = DILATED_GROUPS[grp][1]
        if dilation == 1:
            outs.append(o_ref[0, 0].astype(F32))
            lses.append(l_ref[0, 0])
            continue
        for r in range(dilation):
            rows = pl.ds(r, tile // dilation, stride=dilation)
            stat_ref[grp, rows, :] = l_ref[0, r]
            for c in range(n_chunks):
                rows_ref[grp, c, rows, :] = o_ref[0, r, :, c * LANES:(c + 1) * LANES].astype(F32)
        outs.append(jnp.concatenate([rows_ref[grp, c] for c in range(n_chunks)], axis=1))
        lses.append(stat_ref[grp])
    top = jnp.maximum(jnp.maximum(lses[0], lses[1]), lses[2])
    es = [jnp.exp(l - top) for l in lses]
    den = es[0] + es[1] + es[2]
    expand = e_ref[...]
    y = None
    for grp in range(N_GROUPS):
        hi, lo = _split_bf16(es[grp] / den)
        term = (_dot(hi, expand) + _dot(lo, expand)) * outs[grp]
        y = term if y is None else y + term
    h_new = h_ref[0] + _dot(y.astype(BF16), w_ref[...])
    out_ref[0] = h_new
    hn_ref[0] = _rmsnorm_rows(h_new, g_ref[...]).astype(hn_ref.dtype)


def _dilated_out_proj(outs, lses, w_out, h, g_next, batch, seq):
    tm = TM_EPILOGUE
    expand = (np.arange(LANES)[:, None] == (np.arange(D_ATTN)[None, :] // HEAD_DIM)).astype(np.float32)

    def by_residue(width, dilation):
        return pl.BlockSpec((1, dilation, tm // dilation, width), lambda b, i: (b, 0, i, 0))

    def grouped(arrays, width):
        return [a.reshape(batch, d, seq // d, width) for a, (_, d) in zip(arrays, DILATED_GROUPS)]

    row = pl.BlockSpec((1, tm, D_MODEL), lambda b, i: (b, i, 0))
    full = lambda r, c: pl.BlockSpec((r, c), lambda b, i: (0, 0))
    h_new, hn = pl.pallas_call(
        _dil_out_body,
        grid=(batch, seq // tm),
        in_specs=[by_residue(D_ATTN, d) for _, d in DILATED_GROUPS]
        + [by_residue(LANES, d) for _, d in DILATED_GROUPS]
        + [full(LANES, D_ATTN), full(D_ATTN, D_MODEL), row, full(1, D_MODEL)],
        out_specs=[row, row],
        out_shape=[jax.ShapeDtypeStruct((batch, seq, D_MODEL), F32),
                   jax.ShapeDtypeStruct((batch, seq, D_MODEL), BF16)],
        scratch_shapes=[pltpu.VMEM((N_GROUPS, D_ATTN // LANES, tm, LANES), F32),
                        pltpu.VMEM((N_GROUPS, tm, LANES), F32)],
        compiler_params=_params(2),
        name="dilated_out_proj",
    )(*grouped(outs, D_ATTN), *grouped(lses, LANES), jnp.asarray(expand, BF16), w_out,
      h.reshape(batch, seq, D_MODEL), g_next.reshape(1, D_MODEL))
    return h_new.reshape(batch * seq, D_MODEL), hn.reshape(batch * seq, D_MODEL)


def _dilated_mixer(h, g, w_in, w_out, g_next, batch, seq):
    w_in = w_in.astype(BF16)
    parts = []
    for group, hn in enumerate(_rmsnorm_by_residue(h, g, batch, seq)):
        proj = _matmul(hn, w_in, BF16, f"dilated_in_proj_g{group}", first_col=group * 3 * D_ATTN, n_cols=3 * D_ATTN)
        parts.append(_dilated_attention(proj, batch, seq, group))
    return _dilated_out_proj([p[0] for p in parts], [p[1] for p in parts], w_out.astype(BF16), h, g_next,
                             batch, seq)


def _sb_attn_body(q_ref, k_ref, v_ref, tri_ref, o_ref):
    i = pl.program_id(2)
    row = lax.broadcasted_iota(jnp.int32, (SB_CHUNK, SB_CHUNK), 0)
    col = lax.broadcasted_iota(jnp.int32, (SB_CHUNK, SB_CHUNK), 1)
    strict = col < row
    first_head = lax.broadcasted_iota(jnp.int32, (SB_CHUNK, LANES), 1) < HEAD_DIM
    head_lanes = (first_head, ~first_head)
    tri = tri_ref[...]
    heads = range(2)
    q_heads = []
    for half in range(SB_HALVES):
        q_pair = q_ref[0, half * SB_CHUNK:(half + 1) * SB_CHUNK, :] * ATTN_SCALE
        q_heads.append([jnp.where(sel, q_pair, jnp.zeros_like(q_pair)) for sel in head_lanes])

    def chunk(j, state, work):
        start = pl.multiple_of(j * SB_CHUNK, SB_CHUNK)
        k_j = k_ref[0, pl.ds(start, SB_CHUNK), :]
        v_j = v_ref[0, pl.ds(start, SB_CHUNK), :]
        v_heads = [jnp.where(sel, v_j, jnp.zeros_like(v_j)) for sel in head_lanes]
        chains = [(half, hh, diag) for half, diag in work for hh in heads]
        zs = [_dot_nt(q_heads[half][hh], k_j) for half, hh, _ in chains]
        zs = [jnp.where(strict, z, -1e30) if diag else z for z, (_, _, diag) in zip(zs, chains)]
        sps = [jnp.maximum(z, 0.0) + jnp.log(1.0 + jnp.exp2(jnp.abs(z) * NEG_LOG2E)) for z in zs]
        later = [_dot(sp.astype(BF16), tri) for sp in sps]
        atts = [jnp.exp((z - sp) + after + state[half][0][hh]).astype(BF16)
                for z, sp, after, (half, hh, _) in zip(zs, sps, later, chains)]
        new_state = list(state)
        for c, (half, hh, _) in enumerate(chains):
            carries, acc = new_state[half]
            chunk_sum = later[c][:, 0:1] - sps[c][:, 0:1]
            carries = tuple(carries[x] + chunk_sum if x == hh else carries[x] for x in heads)
            new_state[half] = (carries, acc + _dot(atts[c], v_heads[hh]))
        return tuple(new_state)

    def max_carry(state):
        return functools.reduce(jnp.maximum, [jnp.max(c) for carries, _ in state for c in carries])

    zero_carry = jnp.zeros((SB_CHUNK, 1), F32)
    state = (((zero_carry, zero_carry), jnp.zeros((SB_CHUNK, LANES), F32)),) * SB_HALVES
    state = chunk(SB_HALVES * i + 1, state, ((1, True),))
    state = chunk(SB_HALVES * i, state, ((0, True), (1, False)))

    def live(loop):
        jj, _, carry_bound = loop
        return jnp.logical_and(jj < SB_HALVES * i, carry_bound > SB_DEAD_CARRY)

    def step(loop):
        jj, state, _ = loop
        state = chunk(SB_HALVES * i - 1 - jj, state, ((0, False), (1, False)))
        return jj + 1, state, max_carry(state)

    _, state, _ = lax.while_loop(live, step, (jnp.int32(0), state, max_carry(state)))
    for half in range(SB_HALVES):
        o_ref[0, half * SB_CHUNK:(half + 1) * SB_CHUNK, :] = state[half][1].astype(o_ref.dtype)


def _stick_breaking_attention(qkv, batch, seq):
    n_pairs = N_HEADS // 2
    view = qkv.reshape(batch, seq, 3 * D_ATTN)
    q_rows = SB_HALVES * SB_CHUNK
    j = np.arange(SB_CHUNK)
    tri = -(j[:, None] > j[None, :]).astype(np.float32)
    out = pl.pallas_call(
        _sb_attn_body,
        grid=(batch, n_pairs, seq // q_rows),
        in_specs=[pl.BlockSpec((1, q_rows, LANES), lambda b, p, i: (b, i, p)),
                  pl.BlockSpec((1, seq, LANES), lambda b, p, i: (b, 0, n_pairs + p)),
                  pl.BlockSpec((1, seq, LANES), lambda b, p, i: (b, 0, 2 * n_pairs + p)),
                  pl.BlockSpec((SB_CHUNK, SB_CHUNK), lambda b, p, i: (0, 0))],
        out_specs=pl.BlockSpec((1, q_rows, LANES), lambda b, p, i: (b, i, p)),
        out_shape=jax.ShapeDtypeStruct((batch, seq, D_ATTN), BF16),
        compiler_params=_params(3),
        name="stick_breaking_attn",
    )(view, view, view, jnp.asarray(tri, BF16))
    return out.reshape(batch * seq, D_ATTN)


def _norm_router_body(h_ref, g_ref, wh_ref, wl_ref, hn_ref, gate_ref, idx_ref):
    y = _rmsnorm_rows(h_ref[...], g_ref[...])
    _to_row_tiles(y, hn_ref)
    y_hi, y_lo = _split_bf16(y)
    w_hi, w_lo = wh_ref[...], wl_ref[...]
    logits = _dot(y_hi, w_hi) + _dot(y_hi, w_lo) + _dot(y_lo, w_hi)
    lane = lax.broadcasted_iota(jnp.int32, logits.shape, 1)
    neg_inf = F32(-jnp.inf)
    logits = jnp.where(lane < N_EXPERTS, logits, neg_inf)
    v1 = jnp.max(logits, axis=1, keepdims=True)
    i1 = jnp.min(jnp.where(logits == v1, lane, LANES), axis=1, keepdims=True)
    rest = jnp.where(lane == i1, neg_inf, logits)
    v2 = jnp.max(rest, axis=1, keepdims=True)
    i2 = jnp.min(jnp.where(rest == v2, lane, LANES), axis=1, keepdims=True)
    e2 = jnp.exp(v2 - v1)
    w1 = 1.0 / (1.0 + e2)
    w2 = e2 * w1
    gate_ref[...] = jnp.where(lane == 0, w1, jnp.where(lane == 1, w2, 0.0))
    idx_ref[...] = jnp.where(lane == 0, i1, jnp.where(lane == 1, i2, 0))


def _norm_router(h, g, w_router):
    t, d = h.shape
    tm = TM_EPILOGUE
    w_pad = jnp.zeros((d, LANES), F32).at[:, :N_EXPERTS].set(w_router)
    w_hi = w_pad.astype(BF16)
    w_lo = (w_pad - w_hi.astype(F32)).astype(BF16)
    row = lambda w: pl.BlockSpec((tm, w), lambda i: (i, 0))
    full = lambda r, c: pl.BlockSpec((r, c), lambda i: (0, 0))
    return pl.pallas_call(
        _norm_router_body,
        grid=(t // tm,),
        in_specs=[row(d), full(1, d), full(d, LANES), full(d, LANES)],
        out_specs=[pl.BlockSpec((tm * ROW_TILE, LANES), lambda i: (i, 0)), row(LANES), row(LANES)],
        out_shape=[jax.ShapeDtypeStruct((t * ROW_TILE, LANES), F32), jax.ShapeDtypeStruct((t, LANES), F32),
                   jax.ShapeDtypeStruct((t, LANES), jnp.int32)],
        compiler_params=_params(1),
        name="norm_router",
    )(h, g.reshape(1, d), w_hi, w_lo)


def _route_layout(idx, n_slots):
    n_tiles = n_slots // TM_GROUP
    flat = idx[:, :TOP_K].reshape(-1)
    onehot = (flat[:, None] == jnp.arange(N_EXPERTS, dtype=jnp.int32)[None, :]).astype(jnp.int32)
    csum = jnp.cumsum(onehot, axis=0)
    rank = jnp.sum(csum * onehot, axis=1) - 1
    counts = csum[-1]
    padded = ((counts + TM_GROUP - 1) // TM_GROUP) * TM_GROUP
    pad_end = jnp.cumsum(padded)
    dest = (pad_end - padded)[flat] + rank
    tokens = jnp.arange(flat.shape[0], dtype=jnp.int32) // TOP_K
    src_tok = jnp.zeros((n_slots,), jnp.int32).at[dest].set(tokens)
    tile_start = jnp.arange(n_tiles, dtype=jnp.int32) * TM_GROUP
    tile_expert = jnp.minimum(jnp.sum((tile_start[:, None] >= pad_end[None, :]).astype(jnp.int32), axis=1),
                              N_EXPERTS - 1)
    n_used = (pad_end[-1:] // TM_GROUP).astype(jnp.int32)
    return src_tok, dest.reshape(-1, TOP_K), tile_expert, n_used


def _to_row_tiles(rows, out_ref):
    n, d = rows.shape
    for c in range(d // LANES):
        out_ref[pl.ds(c, n, stride=d // LANES), :] = rows[:, c * LANES:(c + 1) * LANES]


def _from_row_tiles(tiles_ref, n):
    return jnp.concatenate([tiles_ref[pl.ds(c, n, stride=ROW_TILE), :] for c in range(ROW_TILE)], axis=1)


def _row_copies(wait, idx_ref, n_rows, src_hbm, dst_for_row, sem):
    def copy(r):
        src = src_hbm.at[pl.ds(pl.multiple_of(idx_ref[0, 0, r] * ROW_TILE, ROW_TILE), ROW_TILE)]
        return pltpu.make_async_copy(src, dst_for_row(r), sem)

    def trip(g, c):
        for u in range(ROW_DMA_UNROLL):
            if wait:
                copy(g * ROW_DMA_UNROLL + u).wait()
            else:
                copy(g * ROW_DMA_UNROLL + u).start(priority=u % 2)
        return c

    lax.fori_loop(0, n_rows // ROW_DMA_UNROLL, trip, 0)


def _prefetched_row_gather(idx_ref, next_idx_ref, n_rows, src_hbm, dst_for_row, sems):
    i = pl.program_id(0)
    cur = i & 1

    @pl.when(i == 0)
    def _():
        _row_copies(False, idx_ref, n_rows, src_hbm, functools.partial(dst_for_row, 0), sems.at[0])

    @pl.when(i + 1 < pl.num_programs(0))
    def _():
        _row_copies(False, next_idx_ref, n_rows, src_hbm, functools.partial(dst_for_row, 1 - cur), sems.at[1 - cur])

    _row_copies(True, idx_ref, n_rows, src_hbm, functools.partial(dst_for_row, cur), sems.at[cur])
    return cur


def _tile_rows(r):
    return pl.ds(pl.multiple_of(r * ROW_TILE, ROW_TILE), ROW_TILE)


def _gather_body(src_ref, next_src_ref, x_hbm, o_ref, buf, sems):
    cur = _prefetched_row_gather(src_ref, next_src_ref, TM_GROUP, x_hbm,
                                 lambda b, r: buf.at[b, _tile_rows(r)], sems)
    o_ref[...] = _from_row_tiles(buf.at[cur], TM_GROUP).astype(o_ref.dtype)


def _gather_rows(x, src_tok):
    d = ROW_TILE * LANES
    n_slots = src_tok.shape[0]
    n_tiles = n_slots // TM_GROUP
    src_idx = src_tok.reshape(n_tiles, 1, TM_GROUP)
    return pl.pallas_call(
        _gather_body,
        grid=(n_tiles,),
        in_specs=[pl.BlockSpec((1, 1, TM_GROUP), lambda i: (i, 0, 0), memory_space=pltpu.SMEM),
                  pl.BlockSpec((1, 1, TM_GROUP), lambda i: (jnp.minimum(i + 1, n_tiles - 1), 0, 0),
                               memory_space=pltpu.SMEM),
                  pl.BlockSpec(memory_space=pl.ANY)],
        out_specs=pl.BlockSpec((TM_GROUP, d), lambda i: (i, 0)),
        out_shape=jax.ShapeDtypeStruct((n_slots, d), BF16),
        scratch_shapes=[pltpu.VMEM((2, TM_GROUP * ROW_TILE, LANES), F32), pltpu.SemaphoreType.DMA((2,))],
        compiler_params=_params(1),
        name="moe_gather",
    )(src_idx, src_idx, x)


def _group_swiglu_body(te_ref, nu_ref, x_ref, w_ref, o_ref):
    used = pl.program_id(0) < nu_ref[0]

    @pl.when(used)
    def _():
        _swiglu_cols(x_ref[...], w_ref.at[0], o_ref)

    @pl.when(jnp.logical_not(used))
    def _():
        o_ref[...] = jnp.zeros_like(o_ref)


def _group_down_body(te_ref, nu_ref, x_ref, w_ref, o_ref):
    used = pl.program_id(0) < nu_ref[0]

    @pl.when(used)
    def _():
        _to_row_tiles(_dot(x_ref[...], w_ref[0]), o_ref)

    @pl.when(jnp.logical_not(used))
    def _():
        o_ref[...] = jnp.zeros_like(o_ref)


def _group_swiglu_up(xs, w_gate_up, tile_expert, n_used):
    n_slots, k = xs.shape
    n_tiles = n_slots // TM_GROUP
    last = lambda t, nu: jnp.minimum(t, nu[0] - 1)
    grid_spec = pltpu.PrefetchScalarGridSpec(
        num_scalar_prefetch=2,
        grid=(n_tiles,),
        in_specs=[pl.BlockSpec((TM_GROUP, k), lambda t, te, nu: (last(t, nu), 0)),
                  pl.BlockSpec((1, k, 2 * D_FF), lambda t, te, nu: (te[last(t, nu)], 0, 0))],
        out_specs=pl.BlockSpec((TM_GROUP, D_FF), lambda t, te, nu: (t, 0)),
    )
    return pl.pallas_call(
        _group_swiglu_body,
        grid_spec=grid_spec,
        out_shape=jax.ShapeDtypeStruct((n_slots, D_FF), BF16),
        compiler_params=_params(1),
        name="moe_swiglu_up",
    )(tile_expert, n_used, xs, w_gate_up)


def _group_down(act, w_down, tile_expert, n_used):
    n_slots, k = act.shape
    n = w_down.shape[2]
    n_tiles = n_slots // TM_GROUP
    last = lambda t, nu: jnp.minimum(t, nu[0] - 1)
    grid_spec = pltpu.PrefetchScalarGridSpec(
        num_scalar_prefetch=2,
        grid=(n_tiles,),
        in_specs=[pl.BlockSpec((TM_GROUP, k), lambda t, te, nu: (last(t, nu), 0)),
                  pl.BlockSpec((1, k, n), lambda t, te, nu: (te[last(t, nu)], 0, 0))],
        out_specs=pl.BlockSpec((TM_GROUP * ROW_TILE, LANES), lambda t, te, nu: (t, 0)),
    )
    return pl.pallas_call(
        _group_down_body,
        grid_spec=grid_spec,
        out_shape=jax.ShapeDtypeStruct((n_slots * ROW_TILE, LANES), F32),
        compiler_params=_params(1),
        name="moe_down",
    )(tile_expert, n_used, act, w_down)


def _combined_rows(slot_ref, next_slot_ref, gate_ref, h_ref, y_hbm, buf, sems):
    cur = _prefetched_row_gather(
        slot_ref, next_slot_ref, TOP_K * TM_COMBINE, y_hbm,
        lambda b, e: buf.at[b, e & 1, _tile_rows(lax.shift_right_logical(e, 1))], sems)
    gates = gate_ref[...]
    best, second = [_from_row_tiles(buf.at[cur, c], TM_COMBINE) for c in range(TOP_K)]
    return h_ref[...] + gates[:, 0:1] * best + gates[:, 1:2] * second


def _moe_combine_body(slot_ref, next_slot_ref, gate_ref, h_ref, y_hbm, out_ref, buf, sems):
    out_ref[...] = _combined_rows(slot_ref, next_slot_ref, gate_ref, h_ref, y_hbm, buf, sems)


def _moe_combine_norm_body(slot_ref, next_slot_ref, gate_ref, h_ref, g_ref, y_hbm, out_ref, buf, sems):
    rows = _combined_rows(slot_ref, next_slot_ref, gate_ref, h_ref, y_hbm, buf, sems)
    out_ref[...] = _rmsnorm_rows(rows, g_ref[...])


def _moe_combine(h, y_slots, slots, gates, g_final=None):
    t, d = h.shape
    tm = TM_COMBINE
    n_tiles = t // tm
    slot_idx = slots.reshape(n_tiles, 1, TOP_K * tm)
    in_specs = [pl.BlockSpec((1, 1, TOP_K * tm), lambda i: (i, 0, 0), memory_space=pltpu.SMEM),
                pl.BlockSpec((1, 1, TOP_K * tm), lambda i: (jnp.minimum(i + 1, n_tiles - 1), 0, 0),
                             memory_space=pltpu.SMEM),
                pl.BlockSpec((tm, LANES), lambda i: (i, 0)),
                pl.BlockSpec((tm, d), lambda i: (i, 0))]
    args = [slot_idx, slot_idx, gates, h]
    if g_final is not None:
        in_specs.append(pl.BlockSpec((1, d), lambda i: (0, 0)))
        args.append(g_final.reshape(1, d))
    return pl.pallas_call(
        _moe_combine_body if g_final is None else _moe_combine_norm_body,
        grid=(n_tiles,),
        in_specs=in_specs + [pl.BlockSpec(memory_space=pl.ANY)],
        out_specs=pl.BlockSpec((tm, d), lambda i: (i, 0)),
        out_shape=jax.ShapeDtypeStruct((t, d), F32),
        scratch_shapes=[pltpu.VMEM((2, TOP_K, tm * ROW_TILE, LANES), F32), pltpu.SemaphoreType.DMA((2,))],
        compiler_params=_params(1),
        name="moe_combine",
    )(*args, y_slots)


def _moe_layer(h, g, w_router, w_gate_up, w_down, g_final=None):
    t = h.shape[0]
    n_slots = t * TOP_K + N_EXPERTS * TM_GROUP
    hn, gates, idx = _norm_router(h, g, w_router)
    src_tok, slots, tile_expert, n_used = _route_layout(idx, n_slots)
    xs = _gather_rows(hn, src_tok)
    act = _group_swiglu_up(xs, w_gate_up, tile_expert, n_used)
    y_slots = _group_down(act, w_down, tile_expert, n_used)
    return _moe_combine(h, y_slots, slots, gates, g_final)


def kernel(x, mix_norm, ffn_norm, dil_w_in, dil_w_out, sb_w_qkv, sb_w_out, ffn_w_gate_up, ffn_w_down,
           moe_w_router, moe_w_gate_up, moe_w_down, final_norm):
    batch, seq, d = x.shape
    depth = mix_norm.shape[0]
    h = x.reshape(batch * seq, d)
    hn = None
    for i in range(depth):
        j = i // 2
        last = i + 1 == depth
        if i % 2 == 0:
            h, hn = _dilated_mixer(h, mix_norm[i], dil_w_in[j], dil_w_out[j], ffn_norm[i], batch, seq)
            act = _swiglu_up(hn, ffn_w_gate_up[j].astype(BF16))
            if last:
                h, hn = _matmul_residual(act, ffn_w_down[j].astype(BF16), h, "ffn_down"), None
            else:
                h, hn = _matmul_residual(act, ffn_w_down[j].astype(BF16), h, "ffn_down", g_next=mix_norm[i + 1])
        else:
            if hn is None:
                hn = _rmsnorm(h, mix_norm[i], BF16)
            qkv = _matmul(hn, sb_w_qkv[j].astype(BF16), BF16, "sb_qkv_proj")
            y = _stick_breaking_attention(qkv, batch, seq)
            h = _matmul_residual(y, sb_w_out[j].astype(BF16), h, "sb_out_proj")
            h = _moe_layer(h, ffn_norm[i], moe_w_router[j], moe_w_gate_up[j].astype(BF16),
                           moe_w_down[j].astype(BF16), g_final=final_norm if last else None)
            hn = None
            if last:
                return h.reshape(batch, seq, d)
    return _rmsnorm(h, final_norm, F32).reshape(batch, seq, d)
```

```python
import functools

import numpy as np
import jax
import jax.numpy as jnp
from jax import lax
from jax.experimental import pallas as pl
from jax.experimental.pallas import tpu as pltpu

D_MODEL = 1024
N_HEADS = 16
HEAD_DIM = 64
D_ATTN = N_HEADS * HEAD_DIM
DILATED_GROUPS = ((128, 1), (512, 4), (2048, 16))
N_GROUPS = len(DILATED_GROUPS)
BLOCK = 128
D_FF = 3584
N_EXPERTS = 8
TOP_K = 2
RMS_EPS = 1e-6
ATTN_SCALE = 0.125
NEG_LOG2E = -float(np.float32(np.log2(np.e)))

LANES = 128
VMEM_LIMIT_BYTES = 56 * 1024 * 1024

TM = 1024
TN = 1024
TN_FF = 512
TM_GROUP = 512
TM_COMBINE = 256
TM_EPILOGUE = 256
SB_CHUNK = 256
SB_HALVES = 2
SB_DEAD_CARRY = -104.0
DIL_PAIRS_PER_STAGE = 4
ROW_DMA_UNROLL = 8
ROW_TILE = D_MODEL // LANES

F32 = jnp.float32
BF16 = jnp.bfloat16

ALIBI_SLOPES = tuple(float(np.float32(2.0 ** (-8.0 * (h + 1) / N_HEADS))) for h in range(N_HEADS))


def _params(n_axes):
    return pltpu.CompilerParams(dimension_semantics=("arbitrary",) * n_axes, vmem_limit_bytes=VMEM_LIMIT_BYTES)


def _dot(a, b):
    return jnp.dot(a, b, preferred_element_type=F32)


def _dot_nt(a, b):
    return lax.dot_general(a, b, (((1,), (1,)), ((), ())), preferred_element_type=F32)


def _split_bf16(x):
    hi = x.astype(BF16)
    lo = (x - hi.astype(F32)).astype(BF16)
    return hi, lo


def _rmsnorm_rows(x, g):
    return x * lax.rsqrt(jnp.mean(x * x, axis=-1, keepdims=True) + RMS_EPS) * g


def _rmsnorm_body(x_ref, g_ref, o_ref):
    o_ref[...] = _rmsnorm_rows(x_ref[...], g_ref[...]).astype(o_ref.dtype)


def _rmsnorm(x, g, out_dtype):
    t, d = x.shape
    return pl.pallas_call(
        _rmsnorm_body,
        grid=(t // TM,),
        in_specs=[pl.BlockSpec((TM, d), lambda i: (i, 0)), pl.BlockSpec((1, d), lambda i: (0, 0))],
        out_specs=pl.BlockSpec((TM, d), lambda i: (i, 0)),
        out_shape=jax.ShapeDtypeStruct((t, d), out_dtype),
        compiler_params=_params(1),
        name="rmsnorm",
    )(x, g.reshape(1, d))


def _rmsnorm_residue_body(x_ref, g_ref, *refs):
    o_refs, lanes_ref = refs[:-1], refs[-1]
    y = _rmsnorm_rows(x_ref[0], g_ref[...])
    n_chunks = y.shape[1] // LANES
    for c in range(n_chunks):
        lanes_ref[c] = y[:, c * LANES:(c + 1) * LANES]
    for o_ref, (_, dilation) in zip(o_refs, DILATED_GROUPS):
        if dilation == 1:
            o_ref[0, 0] = y.astype(o_ref.dtype)
            continue
        rows = TM // dilation
        for r in range(dilation):
            for c in range(n_chunks):
                picked = lanes_ref[c, pl.ds(r, rows, stride=dilation), :]
                o_ref[0, r, :, c * LANES:(c + 1) * LANES] = picked.astype(o_ref.dtype)


def _rmsnorm_by_residue(x, g, batch, seq):
    d_model = x.shape[1]
    tiles = seq // TM
    out_specs, out_shape = [], []
    for _, dilation in DILATED_GROUPS:
        out_specs.append(pl.BlockSpec((1, dilation, TM // dilation, d_model), lambda b, i: (b, 0, i, 0)))
        out_shape.append(jax.ShapeDtypeStruct((batch, dilation, seq // dilation, d_model), BF16))
    outs = pl.pallas_call(
        _rmsnorm_residue_body,
        grid=(batch, tiles),
        in_specs=[pl.BlockSpec((1, TM, d_model), lambda b, i: (b, i, 0)),
                  pl.BlockSpec((1, d_model), lambda b, i: (0, 0))],
        out_specs=out_specs,
        out_shape=out_shape,
        scratch_shapes=[pltpu.VMEM((d_model // LANES, TM, LANES), F32)],
        compiler_params=_params(2),
        name="rmsnorm_by_residue",
    )(x.reshape(batch, seq, d_model), g.reshape(1, d_model))
    return [o.reshape(batch * seq, d_model) for o in outs]


def _mm_body(x_ref, w_ref, o_ref):
    x = x_ref[...]
    for c in range(0, o_ref.shape[1], TN):
        o_ref[:, c:c + TN] = _dot(x, w_ref[:, c:c + TN]).astype(o_ref.dtype)


def _mm_res_body(x_ref, w_ref, r_ref, o_ref):
    o_ref[...] = r_ref[...] + _dot(x_ref[...], w_ref[...])


def _mm_res_norm_body(x_ref, w_ref, r_ref, g_ref, o_ref, hn_ref):
    h_new = r_ref[...] + _dot(x_ref[...], w_ref[...])
    o_ref[...] = h_new
    hn_ref[...] = _rmsnorm_rows(h_new, g_ref[...]).astype(hn_ref.dtype)


def _matmul(x, w, out_dtype, name, first_col=0, n_cols=None):
    m, k = x.shape
    n = w.shape[1] if n_cols is None else n_cols
    panel = first_col // n
    return pl.pallas_call(
        _mm_body,
        grid=(m // TM,),
        in_specs=[pl.BlockSpec((TM, k), lambda i: (i, 0)), pl.BlockSpec((k, n), lambda i: (0, panel))],
        out_specs=pl.BlockSpec((TM, n), lambda i: (i, 0)),
        out_shape=jax.ShapeDtypeStruct((m, n), out_dtype),
        compiler_params=_params(1),
        name=name,
    )(x, w)


def _matmul_residual(x, w, res, name, g_next=None):
    m, k = x.shape
    n = w.shape[1]
    row = pl.BlockSpec((TM, n), lambda i: (i, 0))
    in_specs = [pl.BlockSpec((TM, k), lambda i: (i, 0)), pl.BlockSpec((k, n), lambda i: (0, 0)), row]
    if g_next is None:
        return pl.pallas_call(
            _mm_res_body,
            grid=(m // TM,),
            in_specs=in_specs,
            out_specs=row,
            out_shape=jax.ShapeDtypeStruct((m, n), F32),
            compiler_params=_params(1),
            name=name,
        )(x, w, res)
    return pl.pallas_call(
        _mm_res_norm_body,
        grid=(m // TM,),
        in_specs=in_specs + [pl.BlockSpec((1, n), lambda i: (0, 0))],
        out_specs=[row, row],
        out_shape=[jax.ShapeDtypeStruct((m, n), F32), jax.ShapeDtypeStruct((m, n), BF16)],
        compiler_params=_params(1),
        name=name,
    )(x, w, res, g_next.reshape(1, n))


def _swiglu(g, u):
    return g * (1.0 / (1.0 + jnp.exp(-g))) * u


def _swiglu_cols(x, w_gate_up, o_ref):
    for c in range(0, D_FF, TN_FF):
        gate = _dot(x, w_gate_up[:, c:c + TN_FF])
        up = _dot(x, w_gate_up[:, D_FF + c:D_FF + c + TN_FF])
        o_ref[:, c:c + TN_FF] = _swiglu(gate, up).astype(o_ref.dtype)


def _swiglu_body(x_ref, w_ref, o_ref):
    _swiglu_cols(x_ref[...], w_ref, o_ref)


def _swiglu_up(x, w_gate_up):
    m, k = x.shape
    return pl.pallas_call(
        _swiglu_body,
        grid=(m // TM,),
        in_specs=[pl.BlockSpec((TM, k), lambda i: (i, 0)),
                  pl.BlockSpec((k, 2 * D_FF), lambda i: (0, 0))],
        out_specs=pl.BlockSpec((TM, D_FF), lambda i: (i, 0)),
        out_shape=jax.ShapeDtypeStruct((m, D_FF), BF16),
        compiler_params=_params(1),
        name="swiglu_up",
    )(x, w_gate_up)


def _alibi_window_bias(dilation):
    a = np.arange(BLOCK)[:, None]
    m = np.arange(2 * BLOCK)[None, :]
    rel = BLOCK + a - m
    in_window = (rel >= 0) & (rel <= BLOCK)
    slopes = np.asarray(ALIBI_SLOPES, np.float32)[:, None, None]
    bias = -slopes * (rel * dilation).astype(np.float32)[None]
    later = np.where(in_window[None], bias, -np.inf)
    first = np.where((m >= BLOCK)[None], later, -np.inf)
    return np.stack([first, later]).astype(np.float32)


def _dil_attn_body(q_ref, kc_ref, kp_ref, vc_ref, vp_ref, bias_ref, o_ref, lse_ref):
    n_keys = 2 * BLOCK
    lane = lax.broadcasted_iota(jnp.int32, (BLOCK, LANES), 1)
    first_head = lane < HEAD_DIM
    first_head_keys = lax.broadcasted_iota(jnp.int32, (n_keys, LANES), 1) < HEAD_DIM
    lse_tile = jnp.zeros((BLOCK, LANES), F32)

    def by_head(pair_rows):
        zero = jnp.zeros_like(pair_rows)
        return jnp.concatenate([jnp.where(first_head_keys, pair_rows, zero),
                                jnp.where(first_head_keys, zero, pair_rows)], axis=0)

    for first_pair in range(0, N_HEADS // 2, DIL_PAIRS_PER_STAGE):
        pairs = range(first_pair, first_pair + DIL_PAIRS_PER_STAGE)
        cols = {p: slice(p * LANES, (p + 1) * LANES) for p in pairs}
        scores = {p: _dot_nt(q_ref[:, cols[p]] * ATTN_SCALE,
                             by_head(jnp.concatenate([kp_ref[:, cols[p]], kc_ref[:, cols[p]]], axis=0)))
                  for p in pairs}
        probs, norms = {}, {}
        for p in pairs:
            pair_probs = []
            for hh in range(2):
                h = 2 * p + hh
                s = scores[p][:, hh * n_keys:(hh + 1) * n_keys] + bias_ref[0, h]
                s_max = jnp.max(s, axis=1, keepdims=True)
                e = jnp.exp(s - s_max)
                z = jnp.sum(e, axis=1, keepdims=True)
                lse_tile = jnp.where(lane == h, s_max + jnp.log(z), lse_tile)
                pair_probs.append(e.astype(BF16))
                norms[h] = z
            probs[p] = jnp.concatenate(pair_probs, axis=1)
        for p in pairs:
            v_heads = by_head(jnp.concatenate([vp_ref[:, cols[p]], vc_ref[:, cols[p]]], axis=0))
            o_pair = _dot(probs[p], v_heads) / jnp.where(first_head, norms[2 * p], norms[2 * p + 1])
            o_ref[:, cols[p]] = o_pair.astype(o_ref.dtype)
    lse_ref[...] = lse_tile


def _dilated_attention(proj, batch, seq, group):
    _, dilation = DILATED_GROUPS[group]
    n_blocks = seq // dilation // BLOCK

    def block(which, back):
        return lambda b, r, n: ((b * dilation + r) * n_blocks + jnp.maximum(n - back, 0), which)

    blk = (BLOCK, D_ATTN)
    bias = jnp.asarray(_alibi_window_bias(dilation))
    return pl.pallas_call(
        _dil_attn_body,
        grid=(batch, dilation, n_blocks),
        in_specs=[pl.BlockSpec(blk, block(0, 0)), pl.BlockSpec(blk, block(1, 0)), pl.BlockSpec(blk, block(1, 1)),
                  pl.BlockSpec(blk, block(2, 0)), pl.BlockSpec(blk, block(2, 1)),
                  pl.BlockSpec((1,) + bias.shape[1:], lambda b, r, n: (jnp.minimum(n, 1), 0, 0, 0))],
        out_specs=[pl.BlockSpec(blk, block(0, 0)), pl.BlockSpec((BLOCK, LANES), block(0, 0))],
        out_shape=[jax.ShapeDtypeStruct((batch * seq, D_ATTN), BF16),
                   jax.ShapeDtypeStruct((batch * seq, LANES), F32)],
        compiler_params=_params(3),
        name=f"dilated_attn_g{group}",
    )(proj, proj, proj, proj, proj, bias)


def _dil_out_body(o0_ref, o1_ref, o2_ref, l0_ref, l1_ref, l2_ref, e_ref, w_ref, h_ref, g_ref, out_ref, hn_ref,
                  rows_ref, stat_ref):
    tile = h_ref.shape[1]
    n_chunks = D_ATTN // LANES
    outs, lses = [], []
    for grp, (o_ref, l_ref) in enumerate(zip((o0_ref, o1_ref, o2_ref), (l0_ref, l1_ref, l2_ref))):
        dilation = DILATED_GROUPS[grp][1]
        if dilation == 1:
            outs.append(o_ref[0, 0].astype(F32))
            lses.append(l_ref[0, 0])
            continue
        for r in range(dilation):
            rows = pl.ds(r, tile // dilation, stride=dilation)
            stat_ref[grp, rows, :] = l_ref[0, r]
            for c in range(n_chunks):
                rows_ref[grp, c, rows, :] = o_ref[0, r, :, c * LANES:(c + 1) * LANES].astype(F32)
        outs.append(jnp.concatenate([rows_ref[grp, c] for c in range(n_chunks)], axis=1))
        lses.append(stat_ref[grp])
    top = jnp.maximum(jnp.maximum(lses[0], lses[1]), lses[2])
    es = [jnp.exp(l - top) for l in lses]
    den = es[0] + es[1] + es[2]
    expand = e_ref[...]
    y = None
    for grp in range(N_GROUPS):
        hi, lo = _split_bf16(es[grp] / den)
        term = (_dot(hi, expand) + _dot(lo, expand)) * outs[grp]
        y = term if y is None else y + term
    h_new = h_ref[0] + _dot(y.astype(BF16), w_ref[...])
    out_ref[0] = h_new
    hn_ref[0] = _rmsnorm_rows(h_new, g_ref[...]).astype(hn_ref.dtype)


def _dilated_out_proj(outs, lses, w_out, h, g_next, batch, seq):
    tm = TM_EPILOGUE
    expand = (np.arange(LANES)[:, None] == (np.arange(D_ATTN)[None, :] // HEAD_DIM)).astype(np.float32)

    def by_residue(width, dilation):
        return pl.BlockSpec((1, dilation, tm // dilation, width), lambda b, i: (b, 0, i, 0))

    def grouped(arrays, width):
        return [a.reshape(batch, d, seq // d, width) for a, (_, d) in zip(arrays, DILATED_GROUPS)]

    row = pl.BlockSpec((1, tm, D_MODEL), lambda b, i: (b, i, 0))
    full = lambda r, c: pl.BlockSpec((r, c), lambda b, i: (0, 0))
    h_new, hn = pl.pallas_call(
        _dil_out_body,
        grid=(batch, seq // tm),
        in_specs=[by_residue(D_ATTN, d) for _, d in DILATED_GROUPS]
        + [by_residue(LANES, d) for _, d in DILATED_GROUPS]
        + [full(LANES, D_ATTN), full(D_ATTN, D_MODEL), row, full(1, D_MODEL)],
        out_specs=[row, row],
        out_shape=[jax.ShapeDtypeStruct((batch, seq, D_MODEL), F32),
                   jax.ShapeDtypeStruct((batch, seq, D_MODEL), BF16)],
        scratch_shapes=[pltpu.VMEM((N_GROUPS, D_ATTN // LANES, tm, LANES), F32),
                        pltpu.VMEM((N_GROUPS, tm, LANES), F32)],
        compiler_params=_params(2),
        name="dilated_out_proj",
    )(*grouped(outs, D_ATTN), *grouped(lses, LANES), jnp.asarray(expand, BF16), w_out,
      h.reshape(batch, seq, D_MODEL), g_next.reshape(1, D_MODEL))
    return h_new.reshape(batch * seq, D_MODEL), hn.reshape(batch * seq, D_MODEL)


def _dilated_mixer(h, g, w_in, w_out, g_next, batch, seq):
    w_in = w_in.astype(BF16)
    parts = []
    for group, hn in enumerate(_rmsnorm_by_residue(h, g, batch, seq)):
        proj = _matmul(hn, w_in, BF16, f"dilated_in_proj_g{group}", first_col=group * 3 * D_ATTN, n_cols=3 * D_ATTN)
        parts.append(_dilated_attention(proj, batch, seq, group))
    return _dilated_out_proj([p[0] for p in parts], [p[1] for p in parts], w_out.astype(BF16), h, g_next,
                             batch, seq)


def _sb_attn_body(q_ref, k_ref, v_ref, tri_ref, o_ref):
    i = pl.program_id(2)
    row = lax.broadcasted_iota(jnp.int32, (SB_CHUNK, SB_CHUNK), 0)
    col = lax.broadcasted_iota(jnp.int32, (SB_CHUNK, SB_CHUNK), 1)
    strict = col < row
    first_head = lax.broadcasted_iota(jnp.int32, (SB_CHUNK, LANES), 1) < HEAD_DIM
    head_lanes = (first_head, ~first_head)
    tri = tri_ref[...]
    heads = range(2)
    q_heads = []
    for half in range(SB_HALVES):
        q_pair = q_ref[0, half * SB_CHUNK:(half + 1) * SB_CHUNK, :] * ATTN_SCALE
        q_heads.append([jnp.where(sel, q_pair, jnp.zeros_like(q_pair)) for sel in head_lanes])

    def advance(state, work):
        def rows(ref, j):
            return ref[0, pl.ds(pl.multiple_of(j * SB_CHUNK, SB_CHUNK), SB_CHUNK), :]

        chunks = list({id(item[0]): item[0] for item in work}.values())
        k_of = {id(j): rows(k_ref, j) for j in chunks}
        v_of = {id(j): [jnp.where(sel, v, jnp.zeros_like(v)) for sel in head_lanes]
                for j in chunks for v in [rows(v_ref, j)]}
        chains = [(j, half, hh, diag) for j, half, diag in work for hh in heads]
        zs = [_dot_nt(q_heads[half][hh], k_of[id(j)]) for j, half, hh, _ in chains]
        zs = [jnp.where(strict, z, -1e30) if diag else z for z, (_, _, _, diag) in zip(zs, chains)]
        sps = [jnp.maximum(z, 0.0) + jnp.log(1.0 + jnp.exp2(jnp.abs(z) * NEG_LOG2E)) for z in zs]
        later = [_dot(sp.astype(BF16), tri) for sp in sps]
        new_state = list(state)
        for c, (j, half, hh, _) in enumerate(chains):
            carries, acc = new_state[half]
            att = jnp.exp((zs[c] - sps[c]) + later[c] + carries[hh]).astype(BF16)
            chunk_sum = later[c][:, 0:1] - sps[c][:, 0:1]
            carries = tuple(carries[x] + chunk_sum if x == hh else carries[x] for x in heads)
            new_state[half] = (carries, acc + _dot(att, v_of[id(j)][hh]))
        return tuple(new_state)

    def max_carry(state):
        return functools.reduce(jnp.maximum, [jnp.max(c) for carries, _ in state for c in carries])

    zero_carry = jnp.zeros((SB_CHUNK, 1), F32)
    state = (((zero_carry, zero_carry), jnp.zeros((SB_CHUNK, LANES), F32)),) * SB_HALVES
    diag1, diag0 = SB_HALVES * i + 1, SB_HALVES * i
    state = advance(state, ((diag1, 1, True), (diag0, 0, True), (diag0, 1, False)))

    def live(loop):
        jj, _, carry_bound = loop
        return jnp.logical_and(jj < SB_HALVES * i, carry_bound > SB_DEAD_CARRY)

    def step(loop):
        jj, state, _ = loop
        j = SB_HALVES * i - 1 - jj
        state = advance(state, ((j, 0, False), (j, 1, False)))
        return jj + 1, state, max_carry(state)

    _, state, _ = lax.while_loop(live, step, (jnp.int32(0), state, max_carry(state)))
    for half in range(SB_HALVES):
        o_ref[0, half * SB_CHUNK:(half + 1) * SB_CHUNK, :] = state[half][1].astype(o_ref.dtype)


def _stick_breaking_attention(qkv, batch, seq):
    n_pairs = N_HEADS // 2
    view = qkv.reshape(batch, seq, 3 * D_ATTN)
    q_rows = SB_HALVES * SB_CHUNK
    j = np.arange(SB_CHUNK)
    tri = -(j[:, None] > j[None, :]).astype(np.float32)
    out = pl.pallas_call(
        _sb_attn_body,
        grid=(batch, n_pairs, seq // q_rows),
        in_specs=[pl.BlockSpec((1, q_rows, LANES), lambda b, p, i: (b, i, p)),
                  pl.BlockSpec((1, seq, LANES), lambda b, p, i: (b, 0, n_pairs + p)),
                  pl.BlockSpec((1, seq, LANES), lambda b, p, i: (b, 0, 2 * n_pairs + p)),
                  pl.BlockSpec((SB_CHUNK, SB_CHUNK), lambda b, p, i: (0, 0))],
        out_specs=pl.BlockSpec((1, q_rows, LANES), lambda b, p, i: (b, i, p)),
        out_shape=jax.ShapeDtypeStruct((batch, seq, D_ATTN), BF16),
        compiler_params=_params(3),
        name="stick_breaking_attn",
    )(view, view, view, jnp.asarray(tri, BF16))
    return out.reshape(batch * seq, D_ATTN)


def _norm_router_body(h_ref, g_ref, wh_ref, wl_ref, hn_ref, gate_ref, idx_ref):
    y = _rmsnorm_rows(h_ref[...], g_ref[...])
    _to_row_tiles(y, hn_ref)
    y_hi, y_lo = _split_bf16(y)
    w_hi, w_lo = wh_ref[...], wl_ref[...]
    logits = _dot(y_hi, w_hi) + _dot(y_hi, w_lo) + _dot(y_lo, w_hi)
    lane = lax.broadcasted_iota(jnp.int32, logits.shape, 1)
    neg_inf = F32(-jnp.inf)
    logits = jnp.where(lane < N_EXPERTS, logits, neg_inf)
    v1 = jnp.max(logits, axis=1, keepdims=True)
    i1 = jnp.min(jnp.where(logits == v1, lane, LANES), axis=1, keepdims=True)
    rest = jnp.where(lane == i1, neg_inf, logits)
    v2 = jnp.max(rest, axis=1, keepdims=True)
    i2 = jnp.min(jnp.where(rest == v2, lane, LANES), axis=1, keepdims=True)
    e2 = jnp.exp(v2 - v1)
    w1 = 1.0 / (1.0 + e2)
    w2 = e2 * w1
    gate_ref[...] = jnp.where(lane == 0, w1, jnp.where(lane == 1, w2, 0.0))
    idx_ref[...] = jnp.where(lane == 0, i1, jnp.where(lane == 1, i2, 0))


def _norm_router(h, g, w_router):
    t, d = h.shape
    tm = TM_EPILOGUE
    w_pad = jnp.zeros((d, LANES), F32).at[:, :N_EXPERTS].set(w_router)
    w_hi = w_pad.astype(BF16)
    w_lo = (w_pad - w_hi.astype(F32)).astype(BF16)
    row = lambda w: pl.BlockSpec((tm, w), lambda i: (i, 0))
    full = lambda r, c: pl.BlockSpec((r, c), lambda i: (0, 0))
    return pl.pallas_call(
        _norm_router_body,
        grid=(t // tm,),
        in_specs=[row(d), full(1, d), full(d, LANES), full(d, LANES)],
        out_specs=[pl.BlockSpec((tm * ROW_TILE, LANES), lambda i: (i, 0)), row(LANES), row(LANES)],
        out_shape=[jax.ShapeDtypeStruct((t * ROW_TILE, LANES), F32), jax.ShapeDtypeStruct((t, LANES), F32),
                   jax.ShapeDtypeStruct((t, LANES), jnp.int32)],
        compiler_params=_params(1),
        name="norm_router",
    )(h, g.reshape(1, d), w_hi, w_lo)


def _route_layout(idx, n_slots):
    n_tiles = n_slots // TM_GROUP
    flat = idx[:, :TOP_K].reshape(-1)
    onehot = (flat[:, None] == jnp.arange(N_EXPERTS, dtype=jnp.int32)[None, :]).astype(jnp.int32)
    csum = jnp.cumsum(onehot, axis=0)
    rank = jnp.sum(csum * onehot, axis=1) - 1
    counts = csum[-1]
    padded = ((counts + TM_GROUP - 1) // TM_GROUP) * TM_GROUP
    pad_end = jnp.cumsum(padded)
    group_start = pad_end - padded
    dest = group_start[flat] + rank
    pad_count = padded - counts
    pad_cum = jnp.cumsum(pad_count)
    k = jnp.arange(n_slots - flat.shape[0], dtype=jnp.int32)
    e_k = jnp.minimum(jnp.sum((k[:, None] >= pad_cum[None, :]).astype(jnp.int32), axis=1), N_EXPERTS - 1)
    in_group = (group_start + counts)[e_k] + k - (pad_cum - pad_count)[e_k]
    pad_slots = jnp.where(k < pad_cum[-1], in_group, pad_end[-1] + k - pad_cum[-1])
    tile_start = jnp.arange(n_tiles, dtype=jnp.int32) * TM_GROUP
    tile_expert = jnp.minimum(jnp.sum((tile_start[:, None] >= pad_end[None, :]).astype(jnp.int32), axis=1),
                              N_EXPERTS - 1)
    n_used = (pad_end[-1:] // TM_GROUP).astype(jnp.int32)
    return dest.reshape(-1, TOP_K), pad_slots, tile_expert, n_used


def _to_row_tiles(rows, out_ref):
    n, d = rows.shape
    for c in range(d // LANES):
        out_ref[pl.ds(c, n, stride=d // LANES), :] = rows[:, c * LANES:(c + 1) * LANES]


def _from_row_tiles(tiles_ref, n):
    return jnp.concatenate([tiles_ref[pl.ds(c, n, stride=ROW_TILE), :] for c in range(ROW_TILE)], axis=1)


def _row_copies(wait, idx_ref, n_rows, src_hbm, dst_for_row, sem):
    def copy(r):
        src = src_hbm.at[pl.ds(pl.multiple_of(idx_ref[0, 0, r] * ROW_TILE, ROW_TILE), ROW_TILE)]
        return pltpu.make_async_copy(src, dst_for_row(r), sem)

    def trip(g, c):
        for u in range(ROW_DMA_UNROLL):
            if wait:
                copy(g * ROW_DMA_UNROLL + u).wait()
            else:
                copy(g * ROW_DMA_UNROLL + u).start(priority=u % 2)
        return c

    lax.fori_loop(0, n_rows // ROW_DMA_UNROLL, trip, 0)


def _prefetched_row_gather(idx_ref, next_idx_ref, n_rows, src_hbm, dst_for_row, sems):
    i = pl.program_id(0)
    cur = i & 1

    @pl.when(i == 0)
    def _():
        _row_copies(False, idx_ref, n_rows, src_hbm, functools.partial(dst_for_row, 0), sems.at[0])

    @pl.when(i + 1 < pl.num_programs(0))
    def _():
        _row_copies(False, next_idx_ref, n_rows, src_hbm, functools.partial(dst_for_row, 1 - cur), sems.at[1 - cur])

    _row_copies(True, idx_ref, n_rows, src_hbm, functools.partial(dst_for_row, cur), sems.at[cur])
    return cur


def _tile_rows(r):
    return pl.ds(pl.multiple_of(r * ROW_TILE, ROW_TILE), ROW_TILE)


def _dispatch_body(dest_ref, x_ref, xs_hbm, sem):
    def copy(e):
        src = x_ref.at[_tile_rows(lax.shift_right_logical(e, 1))]
        return pltpu.make_async_copy(src, xs_hbm.at[_tile_rows(dest_ref[0, 0, e])], sem)

    def trip(wait, g, c):
        for u in range(ROW_DMA_UNROLL):
            if wait:
                copy(g * ROW_DMA_UNROLL + u).wait()
            else:
                copy(g * ROW_DMA_UNROLL + u).start(priority=u % 2)
        return c

    n_trips = dest_ref.shape[2] // ROW_DMA_UNROLL
    lax.fori_loop(0, n_trips, functools.partial(trip, False), 0)
    lax.fori_loop(0, n_trips, functools.partial(trip, True), 0)


def _dispatch_rows(x, dest, pad_dest):
    assert TOP_K == 2
    tm = TM_COMBINE
    n_tokens = x.shape[0] // ROW_TILE
    token_steps = n_tokens // tm
    all_dest = jnp.concatenate([dest.reshape(-1), pad_dest]).reshape(-1, 1, TOP_K * tm)
    n_slots = all_dest.size
    return pl.pallas_call(
        _dispatch_body,
        grid=(all_dest.shape[0],),
        in_specs=[pl.BlockSpec((1, 1, TOP_K * tm), lambda i: (i, 0, 0), memory_space=pltpu.SMEM),
                  pl.BlockSpec((tm * ROW_TILE, LANES), lambda i: (jnp.minimum(i, token_steps - 1), 0))],
        out_specs=pl.BlockSpec(memory_space=pl.ANY),
        out_shape=jax.ShapeDtypeStruct((n_slots * ROW_TILE, LANES), F32),
        scratch_shapes=[pltpu.SemaphoreType.DMA(())],
        compiler_params=_params(1),
        name="moe_dispatch",
    )(all_dest, x)


def _group_swiglu_body(te_ref, nu_ref, x_ref, w_ref, o_ref):
    used = pl.program_id(0) < nu_ref[0]

    @pl.when(used)
    def _():
        _swiglu_cols(_from_row_tiles(x_ref, TM_GROUP).astype(BF16), w_ref.at[0], o_ref)

    @pl.when(jnp.logical_not(used))
    def _():
        o_ref[...] = jnp.zeros_like(o_ref)


def _group_down_body(te_ref, nu_ref, x_ref, w_ref, o_ref):
    used = pl.program_id(0) < nu_ref[0]

    @pl.when(used)
    def _():
        _to_row_tiles(_dot(x_ref[...], w_ref[0]), o_ref)

    @pl.when(jnp.logical_not(used))
    def _():
        o_ref[...] = jnp.zeros_like(o_ref)


def _group_swiglu_up(xs, w_gate_up, tile_expert, n_used):
    n_slots = xs.shape[0] // ROW_TILE
    k = ROW_TILE * LANES
    n_tiles = n_slots // TM_GROUP
    last = lambda t, nu: jnp.minimum(t, nu[0] - 1)
    grid_spec = pltpu.PrefetchScalarGridSpec(
        num_scalar_prefetch=2,
        grid=(n_tiles,),
        in_specs=[pl.BlockSpec((TM_GROUP * ROW_TILE, LANES), lambda t, te, nu: (last(t, nu), 0)),
                  pl.BlockSpec((1, k, 2 * D_FF), lambda t, te, nu: (te[last(t, nu)], 0, 0))],
        out_specs=pl.BlockSpec((TM_GROUP, D_FF), lambda t, te, nu: (t, 0)),
    )
    return pl.pallas_call(
        _group_swiglu_body,
        grid_spec=grid_spec,
        out_shape=jax.ShapeDtypeStruct((n_slots, D_FF), BF16),
        compiler_params=_params(1),
        name="moe_swiglu_up",
    )(tile_expert, n_used, xs, w_gate_up)


def _group_down(act, w_down, tile_expert, n_used):
    n_slots, k = act.shape
    n = w_down.shape[2]
    n_tiles = n_slots // TM_GROUP
    last = lambda t, nu: jnp.minimum(t, nu[0] - 1)
    grid_spec = pltpu.PrefetchScalarGridSpec(
        num_scalar_prefetch=2,
        grid=(n_tiles,),
        in_specs=[pl.BlockSpec((TM_GROUP, k), lambda t, te, nu: (last(t, nu), 0)),
                  pl.BlockSpec((1, k, n), lambda t, te, nu: (te[last(t, nu)], 0, 0))],
        out_specs=pl.BlockSpec((TM_GROUP * ROW_TILE, LANES), lambda t, te, nu: (t, 0)),
    )
    return pl.pallas_call(
        _group_down_body,
        grid_spec=grid_spec,
        out_shape=jax.ShapeDtypeStruct((n_slots * ROW_TILE, LANES), F32),
        compiler_params=_params(1),
        name="moe_down",
    )(tile_expert, n_used, act, w_down)


def _combined_rows(slot_ref, next_slot_ref, gate_ref, h_ref, y_hbm, buf, sems):
    cur = _prefetched_row_gather(
        slot_ref, next_slot_ref, TOP_K * TM_COMBINE, y_hbm,
        lambda b, e: buf.at[b, e & 1, _tile_rows(lax.shift_right_logical(e, 1))], sems)
    gates = gate_ref[...]
    best, second = [_from_row_tiles(buf.at[cur, c], TM_COMBINE) for c in range(TOP_K)]
    return h_ref[...] + gates[:, 0:1] * best + gates[:, 1:2] * second


def _moe_combine_body(slot_ref, next_slot_ref, gate_ref, h_ref, y_hbm, out_ref, buf, sems):
    out_ref[...] = _combined_rows(slot_ref, next_slot_ref, gate_ref, h_ref, y_hbm, buf, sems)


def _moe_combine_norm_body(slot_ref, next_slot_ref, gate_ref, h_ref, g_ref, y_hbm, out_ref, buf, sems):
    rows = _combined_rows(slot_ref, next_slot_ref, gate_ref, h_ref, y_hbm, buf, sems)
    out_ref[...] = _rmsnorm_rows(rows, g_ref[...])


def _moe_combine(h, y_slots, slots, gates, g_final=None):
    t, d = h.shape
    tm = TM_COMBINE
    n_tiles = t // tm
    slot_idx = slots.reshape(n_tiles, 1, TOP_K * tm)
    in_specs = [pl.BlockSpec((1, 1, TOP_K * tm), lambda i: (i, 0, 0), memory_space=pltpu.SMEM),
                pl.BlockSpec((1, 1, TOP_K * tm), lambda i: (jnp.minimum(i + 1, n_tiles - 1), 0, 0),
                             memory_space=pltpu.SMEM),
                pl.BlockSpec((tm, LANES), lambda i: (i, 0)),
                pl.BlockSpec((tm, d), lambda i: (i, 0))]
    args = [slot_idx, slot_idx, gates, h]
    if g_final is not None:
        in_specs.append(pl.BlockSpec((1, d), lambda i: (0, 0)))
        args.append(g_final.reshape(1, d))
    return pl.pallas_call(
        _moe_combine_body if g_final is None else _moe_combine_norm_body,
        grid=(n_tiles,),
        in_specs=in_specs + [pl.BlockSpec(memory_space=pl.ANY)],
        out_specs=pl.BlockSpec((tm, d), lambda i: (i, 0)),
        out_shape=jax.ShapeDtypeStruct((t, d), F32),
        scratch_shapes=[pltpu.VMEM((2, TOP_K, tm * ROW_TILE, LANES), F32), pltpu.SemaphoreType.DMA((2,))],
        compiler_params=_params(1),
        name="moe_combine",
    )(*args, y_slots)


def _moe_layer(h, g, w_router, w_gate_up, w_down, g_final=None):
    t = h.shape[0]
    n_slots = t * TOP_K + N_EXPERTS * TM_GROUP
    hn, gates, idx = _norm_router(h, g, w_router)
    slots, pad_slots, tile_expert, n_used = _route_layout(idx, n_slots)
    xs = _dispatch_rows(hn, slots, pad_slots)
    act = _group_swiglu_up(xs, w_gate_up, tile_expert, n_used)
    y_slots = _group_down(act, w_down, tile_expert, n_used)
    return _moe_combine(h, y_slots, slots, gates, g_final)


def kernel(x, mix_norm, ffn_norm, dil_w_in, dil_w_out, sb_w_qkv, sb_w_out, ffn_w_gate_up, ffn_w_down,
           moe_w_router, moe_w_gate_up, moe_w_down, final_norm):
    batch, seq, d = x.shape
    depth = mix_norm.shape[0]
    h = x.reshape(batch * seq, d)
    hn = None
    for i in range(depth):
        j = i // 2
        last = i + 1 == depth
        if i % 2 == 0:
            h, hn = _dilated_mixer(h, mix_norm[i], dil_w_in[j], dil_w_out[j], ffn_norm[i], batch, seq)
            act = _swiglu_up(hn, ffn_w_gate_up[j].astype(BF16))
            if last:
                h, hn = _matmul_residual(act, ffn_w_down[j].astype(BF16), h, "ffn_down"), None
            else:
                h, hn = _matmul_residual(act, ffn_w_down[j].astype(BF16), h, "ffn_down", g_next=mix_norm[i + 1])
        else:
            if hn is None:
                hn = _rmsnorm(h, mix_norm[i], BF16)
            qkv = _matmul(hn, sb_w_qkv[j].astype(BF16), BF16, "sb_qkv_proj")
            y = _stick_breaking_attention(qkv, batch, seq)
            h = _matmul_residual(y, sb_w_out[j].astype(BF16), h, "sb_out_proj")
            h = _moe_layer(h, ffn_norm[i], moe_w_router[j], moe_w_gate_up[j].astype(BF16),
                           moe_w_down[j].astype(BF16), g_final=final_norm if last else None)
            hn = None
            if last:
                return h.reshape(batch, seq, d)
    return _rmsnorm(h, final_norm, F32).reshape(batch, seq, d)
```

```python
import functools

import numpy as np
import jax
import jax.numpy as jnp
from jax import lax
from jax.experimental import pallas as pl
from jax.experimental.pallas import tpu as pltpu

D_MODEL = 1024
N_HEADS = 16
HEAD_DIM = 64
D_ATTN = N_HEADS * HEAD_DIM
DILATED_GROUPS = ((128, 1), (512, 4), (2048, 16))
N_GROUPS = len(DILATED_GROUPS)
BLOCK = 128
D_FF = 3584
N_EXPERTS = 8
TOP_K = 2
RMS_EPS = 1e-6
ATTN_SCALE = 0.125
NEG_LOG2E = -float(np.float32(np.log2(np.e)))

LANES = 128
VMEM_LIMIT_BYTES = 56 * 1024 * 1024

TM = 1024
TN = 1024
TN_FF = 512
TM_GROUP = 512
TM_COMBINE = 256
TM_EPILOGUE = 256
SB_CHUNK = 256
SB_HALVES = 2
SB_DEAD_CARRY = -104.0
DIL_PAIRS_PER_STAGE = 4
ROW_DMA_UNROLL = 8
ROW_TILE = D_MODEL // LANES

F32 = jnp.float32
BF16 = jnp.bfloat16

ALIBI_SLOPES = tuple(float(np.float32(2.0 ** (-8.0 * (h + 1) / N_HEADS))) for h in range(N_HEADS))


def _params(n_axes):
    return pltpu.CompilerParams(dimension_semantics=("arbitrary",) * n_axes, vmem_limit_bytes=VMEM_LIMIT_BYTES)


def _dot(a, b):
    return jnp.dot(a, b, preferred_element_type=F32)


def _dot_nt(a, b):
    return lax.dot_general(a, b, (((1,), (1,)), ((), ())), preferred_element_type=F32)


def _split_bf16(x):
    hi = x.astype(BF16)
    lo = (x - hi.astype(F32)).astype(BF16)
    return hi, lo


def _rmsnorm_rows(x, g):
    return x * lax.rsqrt(jnp.mean(x * x, axis=-1, keepdims=True) + RMS_EPS) * g


def _rmsnorm_body(x_ref, g_ref, o_ref):
    o_ref[...] = _rmsnorm_rows(x_ref[...], g_ref[...]).astype(o_ref.dtype)


def _rmsnorm(x, g, out_dtype):
    t, d = x.shape
    return pl.pallas_call(
        _rmsnorm_body,
        grid=(t // TM,),
        in_specs=[pl.BlockSpec((TM, d), lambda i: (i, 0)), pl.BlockSpec((1, d), lambda i: (0, 0))],
        out_specs=pl.BlockSpec((TM, d), lambda i: (i, 0)),
        out_shape=jax.ShapeDtypeStruct((t, d), out_dtype),
        compiler_params=_params(1),
        name="rmsnorm",
    )(x, g.reshape(1, d))


def _rmsnorm_residue_body(x_ref, g_ref, *refs):
    o_refs, lanes_ref = refs[:-1], refs[-1]
    y = _rmsnorm_rows(x_ref[0], g_ref[...])
    n_chunks = y.shape[1] // LANES
    for c in range(n_chunks):
        lanes_ref[c] = y[:, c * LANES:(c + 1) * LANES]
    for o_ref, (_, dilation) in zip(o_refs, DILATED_GROUPS):
        if dilation == 1:
            o_ref[0, 0] = y.astype(o_ref.dtype)
            continue
        rows = TM // dilation
        for r in range(dilation):
            for c in range(n_chunks):
                picked = lanes_ref[c, pl.ds(r, rows, stride=dilation), :]
                o_ref[0, r, :, c * LANES:(c + 1) * LANES] = picked.astype(o_ref.dtype)


def _rmsnorm_by_residue(x, g, batch, seq):
    d_model = x.shape[1]
    tiles = seq // TM
    out_specs, out_shape = [], []
    for _, dilation in DILATED_GROUPS:
        out_specs.append(pl.BlockSpec((1, dilation, TM // dilation, d_model), lambda b, i: (b, 0, i, 0)))
        out_shape.append(jax.ShapeDtypeStruct((batch, dilation, seq // dilation, d_model), BF16))
    outs = pl.pallas_call(
        _rmsnorm_residue_body,
        grid=(batch, tiles),
        in_specs=[pl.BlockSpec((1, TM, d_model), lambda b, i: (b, i, 0)),
                  pl.BlockSpec((1, d_model), lambda b, i: (0, 0))],
        out_specs=out_specs,
        out_shape=out_shape,
        scratch_shapes=[pltpu.VMEM((d_model // LANES, TM, LANES), F32)],
        compiler_params=_params(2),
        name="rmsnorm_by_residue",
    )(x.reshape(batch, seq, d_model), g.reshape(1, d_model))
    return [o.reshape(batch * seq, d_model) for o in outs]


def _mm_body(x_ref, w_ref, o_ref):
    x = x_ref[...]
    for c in range(0, o_ref.shape[1], TN):
        o_ref[:, c:c + TN] = _dot(x, w_ref[:, c:c + TN]).astype(o_ref.dtype)


def _mm_res_body(x_ref, w_ref, r_ref, o_ref):
    o_ref[...] = r_ref[...] + _dot(x_ref[...], w_ref[...])


def _mm_res_norm_body(x_ref, w_ref, r_ref, g_ref, o_ref, hn_ref):
    h_new = r_ref[...] + _dot(x_ref[...], w_ref[...])
    o_ref[...] = h_new
    hn_ref[...] = _rmsnorm_rows(h_new, g_ref[...]).astype(hn_ref.dtype)


def _matmul(x, w, out_dtype, name, first_col=0, n_cols=None):
    m, k = x.shape
    n = w.shape[1] if n_cols is None else n_cols
    panel = first_col // n
    return pl.pallas_call(
        _mm_body,
        grid=(m // TM,),
        in_specs=[pl.BlockSpec((TM, k), lambda i: (i, 0)), pl.BlockSpec((k, n), lambda i: (0, panel))],
        out_specs=pl.BlockSpec((TM, n), lambda i: (i, 0)),
        out_shape=jax.ShapeDtypeStruct((m, n), out_dtype),
        compiler_params=_params(1),
        name=name,
    )(x, w)


def _matmul_residual(x, w, res, name, g_next=None):
    m, k = x.shape
    n = w.shape[1]
    row = pl.BlockSpec((TM, n), lambda i: (i, 0))
    in_specs = [pl.BlockSpec((TM, k), lambda i: (i, 0)), pl.BlockSpec((k, n), lambda i: (0, 0)), row]
    if g_next is None:
        return pl.pallas_call(
            _mm_res_body,
            grid=(m // TM,),
            in_specs=in_specs,
            out_specs=row,
            out_shape=jax.ShapeDtypeStruct((m, n), F32),
            compiler_params=_params(1),
            name=name,
        )(x, w, res)
    return pl.pallas_call(
        _mm_res_norm_body,
        grid=(m // TM,),
        in_specs=in_specs + [pl.BlockSpec((1, n), lambda i: (0, 0))],
        out_specs=[row, row],
        out_shape=[jax.ShapeDtypeStruct((m, n), F32), jax.ShapeDtypeStruct((m, n), BF16)],
        compiler_params=_params(1),
        name=name,
    )(x, w, res, g_next.reshape(1, n))


def _swiglu(g, u):
    return g * (1.0 / (1.0 + jnp.exp(-g))) * u


def _swiglu_cols(x, w_gate_up, o_ref):
    for c in range(0, D_FF, TN_FF):
        gate = _dot(x, w_gate_up[:, c:c + TN_FF])
        up = _dot(x, w_gate_up[:, D_FF + c:D_FF + c + TN_FF])
        o_ref[:, c:c + TN_FF] = _swiglu(gate, up).astype(o_ref.dtype)


def _swiglu_body(x_ref, w_ref, o_ref):
    _swiglu_cols(x_ref[...], w_ref, o_ref)


def _swiglu_up(x, w_gate_up):
    m, k = x.shape
    return pl.pallas_call(
        _swiglu_body,
        grid=(m // TM,),
        in_specs=[pl.BlockSpec((TM, k), lambda i: (i, 0)),
                  pl.BlockSpec((k, 2 * D_FF), lambda i: (0, 0))],
        out_specs=pl.BlockSpec((TM, D_FF), lambda i: (i, 0)),
        out_shape=jax.ShapeDtypeStruct((m, D_FF), BF16),
        compiler_params=_params(1),
        name="swiglu_up",
    )(x, w_gate_up)


def _alibi_window_bias(dilation):
    a = np.arange(BLOCK)[:, None]
    m = np.arange(2 * BLOCK)[None, :]
    rel = BLOCK + a - m
    in_window = (rel >= 0) & (rel <= BLOCK)
    slopes = np.asarray(ALIBI_SLOPES, np.float32)[:, None, None]
    bias = -slopes * (rel * dilation).astype(np.float32)[None]
    later = np.where(in_window[None], bias, -np.inf)
    first = np.where((m >= BLOCK)[None], later, -np.inf)
    return np.stack([first, later]).astype(np.float32)


def _dil_attn_body(q_ref, kc_ref, kp_ref, vc_ref, vp_ref, bias_ref, o_ref, lse_ref):
    n_keys = 2 * BLOCK
    lane = lax.broadcasted_iota(jnp.int32, (BLOCK, LANES), 1)
    first_head = lane < HEAD_DIM
    first_head_keys = lax.broadcasted_iota(jnp.int32, (n_keys, LANES), 1) < HEAD_DIM
    lse_tile = jnp.zeros((BLOCK, LANES), F32)

    def by_head(pair_rows):
        zero = jnp.zeros_like(pair_rows)
        return jnp.concatenate([jnp.where(first_head_keys, pair_rows, zero),
                                jnp.where(first_head_keys, zero, pair_rows)], axis=0)

    for first_pair in range(0, N_HEADS // 2, DIL_PAIRS_PER_STAGE):
        pairs = range(first_pair, first_pair + DIL_PAIRS_PER_STAGE)
        cols = {p: slice(p * LANES, (p + 1) * LANES) for p in pairs}
        scores = {p: _dot_nt(q_ref[:, cols[p]] * ATTN_SCALE,
                             by_head(jnp.concatenate([kp_ref[:, cols[p]], kc_ref[:, cols[p]]], axis=0)))
                  for p in pairs}
        probs, norms = {}, {}
        for p in pairs:
            pair_probs = []
            for hh in range(2):
                h = 2 * p + hh
                s = scores[p][:, hh * n_keys:(hh + 1) * n_keys] + bias_ref[0, h]
                s_max = jnp.max(s, axis=1, keepdims=True)
                e = jnp.exp(s - s_max)
                z = jnp.sum(e, axis=1, keepdims=True)
                lse_tile = jnp.where(lane == h, s_max + jnp.log(z), lse_tile)
                pair_probs.append(e.astype(BF16))
                norms[h] = z
            probs[p] = jnp.concatenate(pair_probs, axis=1)
        for p in pairs:
            v_heads = by_head(jnp.concatenate([vp_ref[:, cols[p]], vc_ref[:, cols[p]]], axis=0))
            o_pair = _dot(probs[p], v_heads) / jnp.where(first_head, norms[2 * p], norms[2 * p + 1])
            o_ref[:, cols[p]] = o_pair.astype(o_ref.dtype)
    lse_ref[...] = lse_tile


def _dilated_attention(proj, batch, seq, group):
    _, dilation = DILATED_GROUPS[group]
    n_blocks = seq // dilation // BLOCK

    def block(which, back):
        return lambda b, r, n: ((b * dilation + r) * n_blocks + jnp.maximum(n - back, 0), which)

    blk = (BLOCK, D_ATTN)
    bias = jnp.asarray(_alibi_window_bias(dilation))
    return pl.pallas_call(
        _dil_attn_body,
        grid=(batch, dilation, n_blocks),
        in_specs=[pl.BlockSpec(blk, block(0, 0)), pl.BlockSpec(blk, block(1, 0)), pl.BlockSpec(blk, block(1, 1)),
                  pl.BlockSpec(blk, block(2, 0)), pl.BlockSpec(blk, block(2, 1)),
                  pl.BlockSpec((1,) + bias.shape[1:], lambda b, r, n: (jnp.minimum(n, 1), 0, 0, 0))],
        out_specs=[pl.BlockSpec(blk, block(0, 0)), pl.BlockSpec((BLOCK, LANES), block(0, 0))],
        out_shape=[jax.ShapeDtypeStruct((batch * seq, D_ATTN), BF16),
                   jax.ShapeDtypeStruct((batch * seq, LANES), F32)],
        compiler_params=_params(3),
        name=f"dilated_attn_g{group}",
    )(proj, proj, proj, proj, proj, bias)


def _dil_out_body(o0_ref, o1_ref, o2_ref, l0_ref, l1_ref, l2_ref, e_ref, w_ref, h_ref, g_ref, out_ref, hn_ref,
                  rows_ref, stat_ref):
    tile = h_ref.shape[1]
    n_chunks = D_ATTN // LANES
    outs, lses = [], []
    for grp, (o_ref, l_ref) in enumerate(zip((o0_ref, o1_ref, o2_ref), (l0_ref, l1_ref, l2_ref))):
        dilation = DILATED_GROUPS[grp][1]
        if dilation == 1:
            outs.append(o_ref[0, 0].astype(F32))
            lses.append(l_ref[0, 0])
            continue
        for r in range(dilation):
            rows = pl.ds(r, tile // dilation, stride=dilation)
            stat_ref[grp, rows, :] = l_ref[0, r]
            for c in range(n_chunks):
                rows_ref[grp, c, rows, :] = o_ref[0, r, :, c * LANES:(c + 1) * LANES].astype(F32)
        outs.append(jnp.concatenate([rows_ref[grp, c] for c in range(n_chunks)], axis=1))
        lses.append(stat_ref[grp])
    top = jnp.maximum(jnp.maximum(lses[0], lses[1]), lses[2])
    es = [jnp.exp(l - top) for l in lses]
    den = es[0] + es[1] + es[2]
    expand = e_ref[...]
    y = None
    for grp in range(N_GROUPS):
        hi, lo = _split_bf16(es[grp] / den)
        term = (_dot(hi, expand) + _dot(lo, expand)) * outs[grp]
        y = term if y is None else y + term
    h_new = h_ref[0] + _dot(y.astype(BF16), w_ref[...])
    out_ref[0] = h_new
    hn_ref[0] = _rmsnorm_rows(h_new, g_ref[...]).astype(hn_ref.dtype)


def _dilated_out_proj(outs, lses, w_out, h, g_next, batch, seq):
    tm = TM_EPILOGUE
    expand = (np.arange(LANES)[:, None] == (np.arange(D_ATTN)[None, :] // HEAD_DIM)).astype(np.float32)

    def by_residue(width, dilation):
        return pl.BlockSpec((1, dilation, tm // dilation, width), lambda b, i: (b, 0, i, 0))

    def grouped(arrays, width):
        return [a.reshape(batch, d, seq // d, width) for a, (_, d) in zip(arrays, DILATED_GROUPS)]

    row = pl.BlockSpec((1, tm, D_MODEL), lambda b, i: (b, i, 0))
    full = lambda r, c: pl.BlockSpec((r, c), lambda b, i: (0, 0))
    h_new, hn = pl.pallas_call(
        _dil_out_body,
        grid=(batch, seq // tm),
        in_specs=[by_residue(D_ATTN, d) for _, d in DILATED_GROUPS]
        + [by_residue(LANES, d) for _, d in DILATED_GROUPS]
        + [full(LANES, D_ATTN), full(D_ATTN, D_MODEL), row, full(1, D_MODEL)],
        out_specs=[row, row],
        out_shape=[jax.ShapeDtypeStruct((batch, seq, D_MODEL), F32),
                   jax.ShapeDtypeStruct((batch, seq, D_MODEL), BF16)],
        scratch_shapes=[pltpu.VMEM((N_GROUPS, D_ATTN // LANES, tm, LANES), F32),
                        pltpu.VMEM((N_GROUPS, tm, LANES), F32)],
        compiler_params=_params(2),
        name="dilated_out_proj",
    )(*grouped(outs, D_ATTN), *grouped(lses, LANES), jnp.asarray(expand, BF16), w_out,
      h.reshape(batch, seq, D_MODEL), g_next.reshape(1, D_MODEL))
    return h_new.reshape(batch * seq, D_MODEL), hn.reshape(batch * seq, D_MODEL)


def _dilated_mixer(h, g, w_in, w_out, g_next, batch, seq):
    w_in = w_in.astype(BF16)
    parts = []
    for group, hn in enumerate(_rmsnorm_by_residue(h, g, batch, seq)):
        proj = _matmul(hn, w_in, BF16, f"dilated_in_proj_g{group}", first_col=group * 3 * D_ATTN, n_cols=3 * D_ATTN)
        parts.append(_dilated_attention(proj, batch, seq, group))
    return _dilated_out_proj([p[0] for p in parts], [p[1] for p in parts], w_out.astype(BF16), h, g_next,
                             batch, seq)


def _sb_attn_body(q_ref, k_ref, v_ref, tri_ref, o_ref):
    i = pl.program_id(2)
    row = lax.broadcasted_iota(jnp.int32, (SB_CHUNK, SB_CHUNK), 0)
    col = lax.broadcasted_iota(jnp.int32, (SB_CHUNK, SB_CHUNK), 1)
    strict = col < row
    first_head = lax.broadcasted_iota(jnp.int32, (SB_CHUNK, LANES), 1) < HEAD_DIM
    head_lanes = (first_head, ~first_head)
    tri = tri_ref[...]
    heads = range(2)
    q_heads = []
    for half in range(SB_HALVES):
        q_pair = q_ref[0, half * SB_CHUNK:(half + 1) * SB_CHUNK, :] * ATTN_SCALE
        q_heads.append([jnp.where(sel, q_pair, jnp.zeros_like(q_pair)) for sel in head_lanes])

    def advance(state, work):
        def rows(ref, j):
            return ref[0, pl.ds(pl.multiple_of(j * SB_CHUNK, SB_CHUNK), SB_CHUNK), :]

        chunks = list({id(item[0]): item[0] for item in work}.values())
        k_of = {id(j): rows(k_ref, j) for j in chunks}
        v_of = {id(j): [jnp.where(sel, v, jnp.zeros_like(v)) for sel in head_lanes]
                for j in chunks for v in [rows(v_ref, j)]}
        chains = [(j, half, hh, diag) for j, half, diag in work for hh in heads]
        zs = [_dot_nt(q_heads[half][hh], k_of[id(j)]) for j, half, hh, _ in chains]
        zs = [jnp.where(strict, z, -1e30) if diag else z for z, (_, _, _, diag) in zip(zs, chains)]
        sps = [jnp.maximum(z, 0.0) + jnp.log(1.0 + jnp.exp2(jnp.abs(z) * NEG_LOG2E)) for z in zs]
        later = [_dot(sp.astype(BF16), tri) for sp in sps]
        new_state = list(state)
        for c, (j, half, hh, _) in enumerate(chains):
            carries, acc = new_state[half]
            att = jnp.exp((zs[c] - sps[c]) + later[c] + carries[hh]).astype(BF16)
            chunk_sum = later[c][:, 0:1] - sps[c][:, 0:1]
            carries = tuple(carries[x] + chunk_sum if x == hh else carries[x] for x in heads)
            new_state[half] = (carries, acc + _dot(att, v_of[id(j)][hh]))
        return tuple(new_state)

    def max_carry(state):
        return functools.reduce(jnp.maximum, [jnp.max(c) for carries, _ in state for c in carries])

    zero_carry = jnp.zeros((SB_CHUNK, 1), F32)
    state = (((zero_carry, zero_carry), jnp.zeros((SB_CHUNK, LANES), F32)),) * SB_HALVES
    diag1, diag0 = SB_HALVES * i + 1, SB_HALVES * i
    state = advance(state, ((diag1, 1, True), (diag0, 0, True), (diag0, 1, False)))

    def live(loop):
        jj, _, carry_bound = loop
        return jnp.logical_and(jj < SB_HALVES * i, carry_bound > SB_DEAD_CARRY)

    def step(loop):
        jj, state, _ = loop
        j = SB_HALVES * i - 1 - jj
        state = advance(state, ((j, 0, False), (j, 1, False)))
        return jj + 1, state, max_carry(state)

    _, state, _ = lax.while_loop(live, step, (jnp.int32(0), state, max_carry(state)))
    for half in range(SB_HALVES):
        o_ref[0, half * SB_CHUNK:(half + 1) * SB_CHUNK, :] = state[half][1].astype(o_ref.dtype)


def _stick_breaking_attention(qkv, batch, seq):
    n_pairs = N_HEADS // 2
    view = qkv.reshape(batch, seq, 3 * D_ATTN)
    q_rows = SB_HALVES * SB_CHUNK
    j = np.arange(SB_CHUNK)
    tri = -(j[:, None] > j[None, :]).astype(np.float32)
    out = pl.pallas_call(
        _sb_attn_body,
        grid=(batch, n_pairs, seq // q_rows),
        in_specs=[pl.BlockSpec((1, q_rows, LANES), lambda b, p, i: (b, i, p)),
                  pl.BlockSpec((1, seq, LANES), lambda b, p, i: (b, 0, n_pairs + p)),
                  pl.BlockSpec((1, seq, LANES), lambda b, p, i: (b, 0, 2 * n_pairs + p)),
                  pl.BlockSpec((SB_CHUNK, SB_CHUNK), lambda b, p, i: (0, 0))],
        out_specs=pl.BlockSpec((1, q_rows, LANES), lambda b, p, i: (b, i, p)),
        out_shape=jax.ShapeDtypeStruct((batch, seq, D_ATTN), BF16),
        compiler_params=_params(3),
        name="stick_breaking_attn",
    )(view, view, view, jnp.asarray(tri, BF16))
    return out.reshape(batch * seq, D_ATTN)


def _norm_router_body(h_ref, g_ref, wh_ref, wl_ref, hn_ref, gate_ref, idx_ref):
    y = _rmsnorm_rows(h_ref[...], g_ref[...])
    _to_row_tiles(y, hn_ref)
    y_hi, y_lo = _split_bf16(y)
    w_hi, w_lo = wh_ref[...], wl_ref[...]
    logits = _dot(y_hi, w_hi) + _dot(y_hi, w_lo) + _dot(y_lo, w_hi)
    lane = lax.broadcasted_iota(jnp.int32, logits.shape, 1)
    neg_inf = F32(-jnp.inf)
    logits = jnp.where(lane < N_EXPERTS, logits, neg_inf)
    v1 = jnp.max(logits, axis=1, keepdims=True)
    i1 = jnp.min(jnp.where(logits == v1, lane, LANES), axis=1, keepdims=True)
    rest = jnp.where(lane == i1, neg_inf, logits)
    v2 = jnp.max(rest, axis=1, keepdims=True)
    i2 = jnp.min(jnp.where(rest == v2, lane, LANES), axis=1, keepdims=True)
    e2 = jnp.exp(v2 - v1)
    w1 = 1.0 / (1.0 + e2)
    w2 = e2 * w1
    gate_ref[...] = jnp.where(lane == 0, w1, jnp.where(lane == 1, w2, 0.0))
    idx_ref[...] = jnp.where(lane == 0, i1, jnp.where(lane == 1, i2, 0))


def _norm_router(h, g, w_router):
    t, d = h.shape
    tm = TM_EPILOGUE
    w_pad = jnp.zeros((d, LANES), F32).at[:, :N_EXPERTS].set(w_router)
    w_hi = w_pad.astype(BF16)
    w_lo = (w_pad - w_hi.astype(F32)).astype(BF16)
    row = lambda w: pl.BlockSpec((tm, w), lambda i: (i, 0))
    full = lambda r, c: pl.BlockSpec((r, c), lambda i: (0, 0))
    return pl.pallas_call(
        _norm_router_body,
        grid=(t // tm,),
        in_specs=[row(d), full(1, d), full(d, LANES), full(d, LANES)],
        out_specs=[pl.BlockSpec((tm * ROW_TILE, LANES), lambda i: (i, 0)), row(LANES), row(LANES)],
        out_shape=[jax.ShapeDtypeStruct((t * ROW_TILE, LANES), F32), jax.ShapeDtypeStruct((t, LANES), F32),
                   jax.ShapeDtypeStruct((t, LANES), jnp.int32)],
        compiler_params=_params(1),
        name="norm_router",
    )(h, g.reshape(1, d), w_hi, w_lo)


def _route_layout(idx, n_slots):
    n_tiles = n_slots // TM_GROUP
    flat = idx[:, :TOP_K].reshape(-1)
    onehot = (flat[:, None] == jnp.arange(N_EXPERTS, dtype=jnp.int32)[None, :]).astype(jnp.int32)
    csum = jnp.cumsum(onehot, axis=0)
    rank = jnp.sum(csum * onehot, axis=1) - 1
    counts = csum[-1]
    padded = ((counts + TM_GROUP - 1) // TM_GROUP) * TM_GROUP
    pad_end = jnp.cumsum(padded)
    group_start = pad_end - padded
    dest = group_start[flat] + rank
    pad_count = padded - counts
    pad_cum = jnp.cumsum(pad_count)
    k = jnp.arange(n_slots - flat.shape[0], dtype=jnp.int32)
    e_k = jnp.minimum(jnp.sum((k[:, None] >= pad_cum[None, :]).astype(jnp.int32), axis=1), N_EXPERTS - 1)
    in_group = (group_start + counts)[e_k] + k - (pad_cum - pad_count)[e_k]
    pad_slots = jnp.where(k < pad_cum[-1], in_group, pad_end[-1] + k - pad_cum[-1])
    tile_start = jnp.arange(n_tiles, dtype=jnp.int32) * TM_GROUP
    tile_expert = jnp.minimum(jnp.sum((tile_start[:, None] >= pad_end[None, :]).astype(jnp.int32), axis=1),
                              N_EXPERTS - 1)
    n_used = (pad_end[-1:] // TM_GROUP).astype(jnp.int32)
    return dest.reshape(-1, TOP_K), pad_slots, tile_expert, n_used


def _to_row_tiles(rows, out_ref):
    n, d = rows.shape
    for c in range(d // LANES):
        out_ref[pl.ds(c, n, stride=d // LANES), :] = rows[:, c * LANES:(c + 1) * LANES]


def _from_row_tiles(tiles_ref, n):
    return jnp.concatenate([tiles_ref[pl.ds(c, n, stride=ROW_TILE), :] for c in range(ROW_TILE)], axis=1)


def _row_copies(wait, idx_ref, n_rows, src_hbm, dst_for_row, sem):
    def copy(r):
        src = src_hbm.at[pl.ds(pl.multiple_of(idx_ref[0, 0, r] * ROW_TILE, ROW_TILE), ROW_TILE)]
        return pltpu.make_async_copy(src, dst_for_row(r), sem)

    def trip(g, c):
        for u in range(ROW_DMA_UNROLL):
            if wait:
                copy(g * ROW_DMA_UNROLL + u).wait()
            else:
                copy(g * ROW_DMA_UNROLL + u).start(priority=u % 2)
        return c

    lax.fori_loop(0, n_rows // ROW_DMA_UNROLL, trip, 0)


def _prefetched_row_gather(idx_ref, next_idx_ref, n_rows, src_hbm, dst_for_row, sems):
    i = pl.program_id(0)
    cur = i & 1

    @pl.when(i == 0)
    def _():
        _row_copies(False, idx_ref, n_rows, src_hbm, functools.partial(dst_for_row, 0), sems.at[0])

    @pl.when(i + 1 < pl.num_programs(0))
    def _():
        _row_copies(False, next_idx_ref, n_rows, src_hbm, functools.partial(dst_for_row, 1 - cur), sems.at[1 - cur])

    _row_copies(True, idx_ref, n_rows, src_hbm, functools.partial(dst_for_row, cur), sems.at[cur])
    return cur


def _tile_rows(r):
    return pl.ds(pl.multiple_of(r * ROW_TILE, ROW_TILE), ROW_TILE)


def _dispatch_body(dest_ref, x_ref, xs_hbm, sem):
    def copy(e):
        src = x_ref.at[_tile_rows(lax.shift_right_logical(e, 1))]
        return pltpu.make_async_copy(src, xs_hbm.at[_tile_rows(dest_ref[0, 0, e])], sem)

    def trip(wait, g, c):
        for u in range(ROW_DMA_UNROLL):
            if wait:
                copy(g * ROW_DMA_UNROLL + u).wait()
            else:
                copy(g * ROW_DMA_UNROLL + u).start(priority=u % 2)
        return c

    n_trips = dest_ref.shape[2] // ROW_DMA_UNROLL
    lax.fori_loop(0, n_trips, functools.partial(trip, False), 0)
    lax.fori_loop(0, n_trips, functools.partial(trip, True), 0)


def _dispatch_rows(x, dest, pad_dest):
    assert TOP_K == 2
    tm = TM_COMBINE
    n_tokens = x.shape[0] // ROW_TILE
    token_steps = n_tokens // tm
    all_dest = jnp.concatenate([dest.reshape(-1), pad_dest]).reshape(-1, 1, TOP_K * tm)
    n_slots = all_dest.size
    return pl.pallas_call(
        _dispatch_body,
        grid=(all_dest.shape[0],),
        in_specs=[pl.BlockSpec((1, 1, TOP_K * tm), lambda i: (i, 0, 0), memory_space=pltpu.SMEM),
                  pl.BlockSpec((tm * ROW_TILE, LANES), lambda i: (jnp.minimum(i, token_steps - 1), 0))],
        out_specs=pl.BlockSpec(memory_space=pl.ANY),
        out_shape=jax.ShapeDtypeStruct((n_slots * ROW_TILE, LANES), F32),
        scratch_shapes=[pltpu.SemaphoreType.DMA(())],
        compiler_params=_params(1),
        name="moe_dispatch",
    )(all_dest, x)


def _stage_expert_weight(te_ref, w_hbm, layer, w_bf, stage, sems, chunk_of):
    t = pl.program_id(0)
    expert = te_ref[t]
    n_chunks = w_bf.size // stage[0].size

    @pl.when(jnp.logical_or(t == 0, expert != te_ref[jnp.maximum(t - 1, 0)]))
    def _():
        def copy(c):
            return pltpu.make_async_copy(chunk_of(w_hbm.at[layer, expert], c), stage.at[c % 2], sems.at[c % 2])

        copy(0).start()
        for c in range(n_chunks):
            if c + 1 < n_chunks:
                copy(c + 1).start()
            copy(c).wait()
            chunk_of(w_bf, c)[...] = stage[c % 2].astype(w_bf.dtype)


def _group_swiglu_body(te_ref, nu_ref, x_ref, w_hbm, o_ref, w_bf, stage, sems, *, layer):
    used = pl.program_id(0) < nu_ref[0]

    @pl.when(used)
    def _():
        _stage_expert_weight(te_ref, w_hbm, layer, w_bf, stage, sems,
                             lambda ref, c: ref.at[:, pl.ds(c * TN_FF, TN_FF)])
        _swiglu_cols(_from_row_tiles(x_ref, TM_GROUP).astype(BF16), w_bf, o_ref)

    @pl.when(jnp.logical_not(used))
    def _():
        o_ref[...] = jnp.zeros_like(o_ref)


def _group_down_body(te_ref, nu_ref, x_ref, w_hbm, o_ref, w_bf, stage, sems, *, layer):
    used = pl.program_id(0) < nu_ref[0]

    @pl.when(used)
    def _():
        rows = stage.shape[1]
        _stage_expert_weight(te_ref, w_hbm, layer, w_bf, stage, sems,
                             lambda ref, c: ref.at[pl.ds(c * rows, rows), :])
        _to_row_tiles(_dot(x_ref[...], w_bf[...]), o_ref)

    @pl.when(jnp.logical_not(used))
    def _():
        o_ref[...] = jnp.zeros_like(o_ref)


def _group_swiglu_up(xs, w_gate_up, layer, tile_expert, n_used):
    n_slots = xs.shape[0] // ROW_TILE
    k = ROW_TILE * LANES
    n_tiles = n_slots // TM_GROUP
    last = lambda t, nu: jnp.minimum(t, nu[0] - 1)
    grid_spec = pltpu.PrefetchScalarGridSpec(
        num_scalar_prefetch=2,
        grid=(n_tiles,),
        in_specs=[pl.BlockSpec((TM_GROUP * ROW_TILE, LANES), lambda t, te, nu: (last(t, nu), 0)),
                  pl.BlockSpec(memory_space=pl.ANY)],
        out_specs=pl.BlockSpec((TM_GROUP, D_FF), lambda t, te, nu: (t, 0)),
        scratch_shapes=[pltpu.VMEM((k, 2 * D_FF), BF16), pltpu.VMEM((2, k, TN_FF), F32),
                        pltpu.SemaphoreType.DMA((2,))],
    )
    return pl.pallas_call(
        functools.partial(_group_swiglu_body, layer=layer),
        grid_spec=grid_spec,
        out_shape=jax.ShapeDtypeStruct((n_slots, D_FF), BF16),
        compiler_params=_params(1),
        name="moe_swiglu_up",
    )(tile_expert, n_used, xs, w_gate_up)


def _group_down(act, w_down, layer, tile_expert, n_used):
    n_slots, k = act.shape
    n = w_down.shape[-1]
    n_tiles = n_slots // TM_GROUP
    last = lambda t, nu: jnp.minimum(t, nu[0] - 1)
    grid_spec = pltpu.PrefetchScalarGridSpec(
        num_scalar_prefetch=2,
        grid=(n_tiles,),
        in_specs=[pl.BlockSpec((TM_GROUP, k), lambda t, te, nu: (last(t, nu), 0)),
                  pl.BlockSpec(memory_space=pl.ANY)],
        out_specs=pl.BlockSpec((TM_GROUP * ROW_TILE, LANES), lambda t, te, nu: (t, 0)),
        scratch_shapes=[pltpu.VMEM((k, n), BF16), pltpu.VMEM((2, TN_FF, n), F32),
                        pltpu.SemaphoreType.DMA((2,))],
    )
    return pl.pallas_call(
        functools.partial(_group_down_body, layer=layer),
        grid_spec=grid_spec,
        out_shape=jax.ShapeDtypeStruct((n_slots * ROW_TILE, LANES), F32),
        compiler_params=_params(1),
        name="moe_down",
    )(tile_expert, n_used, act, w_down)


def _combined_rows(slot_ref, next_slot_ref, gate_ref, h_ref, y_hbm, buf, sems):
    cur = _prefetched_row_gather(
        slot_ref, next_slot_ref, TOP_K * TM_COMBINE, y_hbm,
        lambda b, e: buf.at[b, e & 1, _tile_rows(lax.shift_right_logical(e, 1))], sems)
    gates = gate_ref[...]
    best, second = [_from_row_tiles(buf.at[cur, c], TM_COMBINE) for c in range(TOP_K)]
    return h_ref[...] + gates[:, 0:1] * best + gates[:, 1:2] * second


def _moe_combine_body(slot_ref, next_slot_ref, gate_ref, h_ref, y_hbm, out_ref, buf, sems):
    out_ref[...] = _combined_rows(slot_ref, next_slot_ref, gate_ref, h_ref, y_hbm, buf, sems)


def _moe_combine_norm_body(slot_ref, next_slot_ref, gate_ref, h_ref, g_ref, y_hbm, out_ref, buf, sems):
    rows = _combined_rows(slot_ref, next_slot_ref, gate_ref, h_ref, y_hbm, buf, sems)
    out_ref[...] = _rmsnorm_rows(rows, g_ref[...])


def _moe_combine(h, y_slots, slots, gates, g_final=None):
    t, d = h.shape
    tm = TM_COMBINE
    n_tiles = t // tm
    slot_idx = slots.reshape(n_tiles, 1, TOP_K * tm)
    in_specs = [pl.BlockSpec((1, 1, TOP_K * tm), lambda i: (i, 0, 0), memory_space=pltpu.SMEM),
                pl.BlockSpec((1, 1, TOP_K * tm), lambda i: (jnp.minimum(i + 1, n_tiles - 1), 0, 0),
                             memory_space=pltpu.SMEM),
                pl.BlockSpec((tm, LANES), lambda i: (i, 0)),
                pl.BlockSpec((tm, d), lambda i: (i, 0))]
    args = [slot_idx, slot_idx, gates, h]
    if g_final is not None:
        in_specs.append(pl.BlockSpec((1, d), lambda i: (0, 0)))
        args.append(g_final.reshape(1, d))
    return pl.pallas_call(
        _moe_combine_body if g_final is None else _moe_combine_norm_body,
        grid=(n_tiles,),
        in_specs=in_specs + [pl.BlockSpec(memory_space=pl.ANY)],
        out_specs=pl.BlockSpec((tm, d), lambda i: (i, 0)),
        out_shape=jax.ShapeDtypeStruct((t, d), F32),
        scratch_shapes=[pltpu.VMEM((2, TOP_K, tm * ROW_TILE, LANES), F32), pltpu.SemaphoreType.DMA((2,))],
        compiler_params=_params(1),
        name="moe_combine",
    )(*args, y_slots)


def _moe_layer(h, g, w_router, w_gate_up, w_down, layer, g_final=None):
    t = h.shape[0]
    n_slots = t * TOP_K + N_EXPERTS * TM_GROUP
    hn, gates, idx = _norm_router(h, g, w_router)
    slots, pad_slots, tile_expert, n_used = _route_layout(idx, n_slots)
    xs = _dispatch_rows(hn, slots, pad_slots)
    act = _group_swiglu_up(xs, w_gate_up, layer, tile_expert, n_used)
    y_slots = _group_down(act, w_down, layer, tile_expert, n_used)
    return _moe_combine(h, y_slots, slots, gates, g_final)


def kernel(x, mix_norm, ffn_norm, dil_w_in, dil_w_out, sb_w_qkv, sb_w_out, ffn_w_gate_up, ffn_w_down,
           moe_w_router, moe_w_gate_up, moe_w_down, final_norm):
    batch, seq, d = x.shape
    depth = mix_norm.shape[0]
    h = x.reshape(batch * seq, d)
    hn = None
    for i in range(depth):
        j = i // 2
        last = i + 1 == depth
        if i % 2 == 0:
            h, hn = _dilated_mixer(h, mix_norm[i], dil_w_in[j], dil_w_out[j], ffn_norm[i], batch, seq)
            act = _swiglu_up(hn, ffn_w_gate_up[j].astype(BF16))
            if last:
                h, hn = _matmul_residual(act, ffn_w_down[j].astype(BF16), h, "ffn_down"), None
            else:
                h, hn = _matmul_residual(act, ffn_w_down[j].astype(BF16), h, "ffn_down", g_next=mix_norm[i + 1])
        else:
            if hn is None:
                hn = _rmsnorm(h, mix_norm[i], BF16)
            qkv = _matmul(hn, sb_w_qkv[j].astype(BF16), BF16, "sb_qkv_proj")
            y = _stick_breaking_attention(qkv, batch, seq)
            h = _matmul_residual(y, sb_w_out[j].astype(BF16), h, "sb_out_proj")
            h = _moe_layer(h, ffn_norm[i], moe_w_router[j], moe_w_gate_up, moe_w_down, j,
                           g_final=final_norm if last else None)
            hn = None
            if last:
                return h.reshape(batch, seq, d)
    return _rmsnorm(h, final_norm, F32).reshape(batch, seq, d)
```

```python
import functools

import numpy as np
import jax
import jax.numpy as jnp
from jax import lax
from jax.experimental import pallas as pl
from jax.experimental.pallas import tpu as pltpu

D_MODEL = 1024
N_HEADS = 16
HEAD_DIM = 64
D_ATTN = N_HEADS * HEAD_DIM
DILATED_GROUPS = ((128, 1), (512, 4), (2048, 16))
N_GROUPS = len(DILATED_GROUPS)
BLOCK = 128
D_FF = 3584
N_EXPERTS = 8
TOP_K = 2
RMS_EPS = 1e-6
ATTN_SCALE = 0.125
NEG_LOG2E = -float(np.float32(np.log2(np.e)))

LANES = 128
VMEM_LIMIT_BYTES = 56 * 1024 * 1024

TM = 1024
TN = 1024
TN_FF = 512
TM_GROUP = 512
TM_COMBINE = 256
TM_EPILOGUE = 256
SB_CHUNK = 256
SB_HALVES = 2
SB_DEAD_CARRY = -104.0
DIL_PAIRS_PER_STAGE = 4
ROW_DMA_UNROLL = 8
ROW_TILE = D_MODEL // LANES
WEIGHT_STAGE_BUFFERS = 4

F32 = jnp.float32
BF16 = jnp.bfloat16

ALIBI_SLOPES = tuple(float(np.float32(2.0 ** (-8.0 * (h + 1) / N_HEADS))) for h in range(N_HEADS))


def _params(n_axes):
    return pltpu.CompilerParams(dimension_semantics=("arbitrary",) * n_axes, vmem_limit_bytes=VMEM_LIMIT_BYTES)


def _dot(a, b):
    return jnp.dot(a, b, preferred_element_type=F32)


def _dot_nt(a, b):
    return lax.dot_general(a, b, (((1,), (1,)), ((), ())), preferred_element_type=F32)


def _split_bf16(x):
    hi = x.astype(BF16)
    lo = (x - hi.astype(F32)).astype(BF16)
    return hi, lo


def _rmsnorm_rows(x, g):
    return x * lax.rsqrt(jnp.mean(x * x, axis=-1, keepdims=True) + RMS_EPS) * g


def _rmsnorm_body(x_ref, g_ref, o_ref):
    o_ref[...] = _rmsnorm_rows(x_ref[...], g_ref[...]).astype(o_ref.dtype)


def _rmsnorm(x, g, out_dtype):
    t, d = x.shape
    return pl.pallas_call(
        _rmsnorm_body,
        grid=(t // TM,),
        in_specs=[pl.BlockSpec((TM, d), lambda i: (i, 0)), pl.BlockSpec((1, d), lambda i: (0, 0))],
        out_specs=pl.BlockSpec((TM, d), lambda i: (i, 0)),
        out_shape=jax.ShapeDtypeStruct((t, d), out_dtype),
        compiler_params=_params(1),
        name="rmsnorm",
    )(x, g.reshape(1, d))


def _rmsnorm_residue_body(x_ref, g_ref, *refs):
    o_refs, lanes_ref = refs[:-1], refs[-1]
    y = _rmsnorm_rows(x_ref[0], g_ref[...])
    n_chunks = y.shape[1] // LANES
    for c in range(n_chunks):
        lanes_ref[c] = y[:, c * LANES:(c + 1) * LANES]
    for o_ref, (_, dilation) in zip(o_refs, DILATED_GROUPS):
        if dilation == 1:
            o_ref[0, 0] = y.astype(o_ref.dtype)
            continue
        rows = TM // dilation
        for r in range(dilation):
            for c in range(n_chunks):
                picked = lanes_ref[c, pl.ds(r, rows, stride=dilation), :]
                o_ref[0, r, :, c * LANES:(c + 1) * LANES] = picked.astype(o_ref.dtype)


def _rmsnorm_by_residue(x, g, batch, seq):
    d_model = x.shape[1]
    tiles = seq // TM
    out_specs, out_shape = [], []
    for _, dilation in DILATED_GROUPS:
        out_specs.append(pl.BlockSpec((1, dilation, TM // dilation, d_model), lambda b, i: (b, 0, i, 0)))
        out_shape.append(jax.ShapeDtypeStruct((batch, dilation, seq // dilation, d_model), BF16))
    outs = pl.pallas_call(
        _rmsnorm_residue_body,
        grid=(batch, tiles),
        in_specs=[pl.BlockSpec((1, TM, d_model), lambda b, i: (b, i, 0)),
                  pl.BlockSpec((1, d_model), lambda b, i: (0, 0))],
        out_specs=out_specs,
        out_shape=out_shape,
        scratch_shapes=[pltpu.VMEM((d_model // LANES, TM, LANES), F32)],
        compiler_params=_params(2),
        name="rmsnorm_by_residue",
    )(x.reshape(batch, seq, d_model), g.reshape(1, d_model))
    return [o.reshape(batch * seq, d_model) for o in outs]


def _mm_body(x_ref, w_ref, o_ref):
    x = x_ref[...]
    for c in range(0, o_ref.shape[1], TN):
        o_ref[:, c:c + TN] = _dot(x, w_ref[:, c:c + TN]).astype(o_ref.dtype)


def _mm_res_body(x_ref, w_ref, r_ref, o_ref):
    o_ref[...] = r_ref[...] + _dot(x_ref[...], w_ref[...])


def _mm_res_norm_body(x_ref, w_ref, r_ref, g_ref, o_ref, hn_ref):
    h_new = r_ref[...] + _dot(x_ref[...], w_ref[...])
    o_ref[...] = h_new
    hn_ref[...] = _rmsnorm_rows(h_new, g_ref[...]).astype(hn_ref.dtype)


def _matmul(x, w, out_dtype, name, first_col=0, n_cols=None):
    m, k = x.shape
    n = w.shape[1] if n_cols is None else n_cols
    panel = first_col // n
    return pl.pallas_call(
        _mm_body,
        grid=(m // TM,),
        in_specs=[pl.BlockSpec((TM, k), lambda i: (i, 0)), pl.BlockSpec((k, n), lambda i: (0, panel))],
        out_specs=pl.BlockSpec((TM, n), lambda i: (i, 0)),
        out_shape=jax.ShapeDtypeStruct((m, n), out_dtype),
        compiler_params=_params(1),
        name=name,
    )(x, w)


def _matmul_residual(x, w, res, name, g_next=None):
    m, k = x.shape
    n = w.shape[1]
    row = pl.BlockSpec((TM, n), lambda i: (i, 0))
    in_specs = [pl.BlockSpec((TM, k), lambda i: (i, 0)), pl.BlockSpec((k, n), lambda i: (0, 0)), row]
    if g_next is None:
        return pl.pallas_call(
            _mm_res_body,
            grid=(m // TM,),
            in_specs=in_specs,
            out_specs=row,
            out_shape=jax.ShapeDtypeStruct((m, n), F32),
            compiler_params=_params(1),
            name=name,
        )(x, w, res)
    return pl.pallas_call(
        _mm_res_norm_body,
        grid=(m // TM,),
        in_specs=in_specs + [pl.BlockSpec((1, n), lambda i: (0, 0))],
        out_specs=[row, row],
        out_shape=[jax.ShapeDtypeStruct((m, n), F32), jax.ShapeDtypeStruct((m, n), BF16)],
        compiler_params=_params(1),
        name=name,
    )(x, w, res, g_next.reshape(1, n))


def _swiglu(g, u):
    return g * (1.0 / (1.0 + jnp.exp(-g))) * u


def _swiglu_cols(x, w_gate_up, o_ref):
    for c in range(0, D_FF, TN_FF):
        gate = _dot(x, w_gate_up[:, c:c + TN_FF])
        up = _dot(x, w_gate_up[:, D_FF + c:D_FF + c + TN_FF])
        o_ref[:, c:c + TN_FF] = _swiglu(gate, up).astype(o_ref.dtype)


def _swiglu_body(x_ref, w_ref, o_ref):
    _swiglu_cols(x_ref[...], w_ref, o_ref)


def _swiglu_up(x, w_gate_up):
    m, k = x.shape
    return pl.pallas_call(
        _swiglu_body,
        grid=(m // TM,),
        in_specs=[pl.BlockSpec((TM, k), lambda i: (i, 0)),
                  pl.BlockSpec((k, 2 * D_FF), lambda i: (0, 0))],
        out_specs=pl.BlockSpec((TM, D_FF), lambda i: (i, 0)),
        out_shape=jax.ShapeDtypeStruct((m, D_FF), BF16),
        compiler_params=_params(1),
        name="swiglu_up",
    )(x, w_gate_up)


def _alibi_window_bias(dilation):
    a = np.arange(BLOCK)[:, None]
    m = np.arange(2 * BLOCK)[None, :]
    rel = BLOCK + a - m
    in_window = (rel >= 0) & (rel <= BLOCK)
    slopes = np.asarray(ALIBI_SLOPES, np.float32)[:, None, None]
    bias = -slopes * (rel * dilation).astype(np.float32)[None]
    later = np.where(in_window[None], bias, -np.inf)
    first = np.where((m >= BLOCK)[None], later, -np.inf)
    return np.stack([first, later]).astype(np.float32)


def _dil_attn_body(q_ref, kc_ref, kp_ref, vc_ref, vp_ref, bias_ref, o_ref, lse_ref):
    n_keys = 2 * BLOCK
    lane = lax.broadcasted_iota(jnp.int32, (BLOCK, LANES), 1)
    first_head = lane < HEAD_DIM
    first_head_keys = lax.broadcasted_iota(jnp.int32, (n_keys, LANES), 1) < HEAD_DIM
    lse_tile = jnp.zeros((BLOCK, LANES), F32)

    def by_head(pair_rows):
        zero = jnp.zeros_like(pair_rows)
        return jnp.concatenate([jnp.where(first_head_keys, pair_rows, zero),
                                jnp.where(first_head_keys, zero, pair_rows)], axis=0)

    for first_pair in range(0, N_HEADS // 2, DIL_PAIRS_PER_STAGE):
        pairs = range(first_pair, first_pair + DIL_PAIRS_PER_STAGE)
        cols = {p: slice(p * LANES, (p + 1) * LANES) for p in pairs}
        scores = {p: _dot_nt(q_ref[:, cols[p]] * ATTN_SCALE,
                             by_head(jnp.concatenate([kp_ref[:, cols[p]], kc_ref[:, cols[p]]], axis=0)))
                  for p in pairs}
        probs, norms = {}, {}
        for p in pairs:
            pair_probs = []
            for hh in range(2):
                h = 2 * p + hh
                s = scores[p][:, hh * n_keys:(hh + 1) * n_keys] + bias_ref[0, h]
                s_max = jnp.max(s, axis=1, keepdims=True)
                e = jnp.exp(s - s_max)
                z = jnp.sum(e, axis=1, keepdims=True)
                lse_tile = jnp.where(lane == h, s_max + jnp.log(z), lse_tile)
                pair_probs.append(e.astype(BF16))
                norms[h] = z
            probs[p] = jnp.concatenate(pair_probs, axis=1)
        for p in pairs:
            v_heads = by_head(jnp.concatenate([vp_ref[:, cols[p]], vc_ref[:, cols[p]]], axis=0))
            o_pair = _dot(probs[p], v_heads) / jnp.where(first_head, norms[2 * p], norms[2 * p + 1])
            o_ref[:, cols[p]] = o_pair.astype(o_ref.dtype)
    lse_ref[...] = lse_tile


def _dilated_attention(proj, batch, seq, group):
    _, dilation = DILATED_GROUPS[group]
    n_blocks = seq // dilation // BLOCK

    def block(which, back):
        return lambda b, r, n: ((b * dilation + r) * n_blocks + jnp.maximum(n - back, 0), which)

    blk = (BLOCK, D_ATTN)
    bias = jnp.asarray(_alibi_window_bias(dilation))
    return pl.pallas_call(
        _dil_attn_body,
        grid=(batch, dilation, n_blocks),
        in_specs=[pl.BlockSpec(blk, block(0, 0)), pl.BlockSpec(blk, block(1, 0)), pl.BlockSpec(blk, block(1, 1)),
                  pl.BlockSpec(blk, block(2, 0)), pl.BlockSpec(blk, block(2, 1)),
                  pl.BlockSpec((1,) + bias.shape[1:], lambda b, r, n: (jnp.minimum(n, 1), 0, 0, 0))],
        out_specs=[pl.BlockSpec(blk, block(0, 0)), pl.BlockSpec((BLOCK, LANES), block(0, 0))],
        out_shape=[jax.ShapeDtypeStruct((batch * seq, D_ATTN), BF16),
                   jax.ShapeDtypeStruct((batch * seq, LANES), F32)],
        compiler_params=_params(3),
        name=f"dilated_attn_g{group}",
    )(proj, proj, proj, proj, proj, bias)


def _dil_out_body(o0_ref, o1_ref, o2_ref, l0_ref, l1_ref, l2_ref, e_ref, w_ref, h_ref, g_ref, out_ref, hn_ref,
                  rows_ref, stat_ref):
    tile = h_ref.shape[1]
    n_chunks = D_ATTN // LANES
    outs, lses = [], []
    for grp, (o_ref, l_ref) in enumerate(zip((o0_ref, o1_ref, o2_ref), (l0_ref, l1_ref, l2_ref))):
        dilation = DILATED_GROUPS[grp][1]
        if dilation == 1:
            outs.append(o_ref[0, 0].astype(F32))
            lses.append(l_ref[0, 0])
            continue
        for r in range(dilation):
            rows = pl.ds(r, tile // dilation, stride=dilation)
            stat_ref[grp, rows, :] = l_ref[0, r]
            for c in range(n_chunks):
                rows_ref[grp, c, rows, :] = o_ref[0, r, :, c * LANES:(c + 1) * LANES].astype(F32)
        outs.append(jnp.concatenate([rows_ref[grp, c] for c in range(n_chunks)], axis=1))
        lses.append(stat_ref[grp])
    top = jnp.maximum(jnp.maximum(lses[0], lses[1]), lses[2])
    es = [jnp.exp(l - top) for l in lses]
    den = es[0] + es[1] + es[2]
    expand = e_ref[...]
    y = None
    for grp in range(N_GROUPS):
        term = _dot((es[grp] / den).astype(BF16), expand) * outs[grp]
        y = term if y is None else y + term
    h_new = h_ref[0] + _dot(y.astype(BF16), w_ref[...])
    out_ref[0] = h_new
    hn_ref[0] = _rmsnorm_rows(h_new, g_ref[...]).astype(hn_ref.dtype)


def _dilated_out_proj(outs, lses, w_out, h, g_next, batch, seq):
    tm = TM_EPILOGUE
    expand = (np.arange(LANES)[:, None] == (np.arange(D_ATTN)[None, :] // HEAD_DIM)).astype(np.float32)

    def by_residue(width, dilation):
        return pl.BlockSpec((1, dilation, tm // dilation, width), lambda b, i: (b, 0, i, 0))

    def grouped(arrays, width):
        return [a.reshape(batch, d, seq // d, width) for a, (_, d) in zip(arrays, DILATED_GROUPS)]

    row = pl.BlockSpec((1, tm, D_MODEL), lambda b, i: (b, i, 0))
    full = lambda r, c: pl.BlockSpec((r, c), lambda b, i: (0, 0))
    h_new, hn = pl.pallas_call(
        _dil_out_body,
        grid=(batch, seq // tm),
        in_specs=[by_residue(D_ATTN, d) for _, d in DILATED_GROUPS]
        + [by_residue(LANES, d) for _, d in DILATED_GROUPS]
        + [full(LANES, D_ATTN), full(D_ATTN, D_MODEL), row, full(1, D_MODEL)],
        out_specs=[row, row],
        out_shape=[jax.ShapeDtypeStruct((batch, seq, D_MODEL), F32),
                   jax.ShapeDtypeStruct((batch, seq, D_MODEL), BF16)],
        scratch_shapes=[pltpu.VMEM((N_GROUPS, D_ATTN // LANES, tm, LANES), F32),
                        pltpu.VMEM((N_GROUPS, tm, LANES), F32)],
        compiler_params=_params(2),
        name="dilated_out_proj",
    )(*grouped(outs, D_ATTN), *grouped(lses, LANES), jnp.asarray(expand, BF16), w_out,
      h.reshape(batch, seq, D_MODEL), g_next.reshape(1, D_MODEL))
    return h_new.reshape(batch * seq, D_MODEL), hn.reshape(batch * seq, D_MODEL)


def _dilated_mixer(h, g, w_in, w_out, g_next, batch, seq):
    w_in = w_in.astype(BF16)
    parts = []
    for group, hn in enumerate(_rmsnorm_by_residue(h, g, batch, seq)):
        proj = _matmul(hn, w_in, BF16, f"dilated_in_proj_g{group}", first_col=group * 3 * D_ATTN, n_cols=3 * D_ATTN)
        parts.append(_dilated_attention(proj, batch, seq, group))
    return _dilated_out_proj([p[0] for p in parts], [p[1] for p in parts], w_out.astype(BF16), h, g_next,
                             batch, seq)


def _sb_attn_body(q_ref, k_ref, v_ref, tri_ref, o_ref):
    i = pl.program_id(2)
    row = lax.broadcasted_iota(jnp.int32, (SB_CHUNK, SB_CHUNK), 0)
    col = lax.broadcasted_iota(jnp.int32, (SB_CHUNK, SB_CHUNK), 1)
    strict = col < row
    first_head = lax.broadcasted_iota(jnp.int32, (SB_CHUNK, LANES), 1) < HEAD_DIM
    head_lanes = (first_head, ~first_head)
    tri = tri_ref[...]
    heads = range(2)
    q_heads = []
    for half in range(SB_HALVES):
        q_pair = q_ref[0, half * SB_CHUNK:(half + 1) * SB_CHUNK, :] * ATTN_SCALE
        q_heads.append([jnp.where(sel, q_pair, jnp.zeros_like(q_pair)) for sel in head_lanes])

    def advance(state, work):
        def rows(ref, j):
            return ref[0, pl.ds(pl.multiple_of(j * SB_CHUNK, SB_CHUNK), SB_CHUNK), :]

        chunks = list({id(item[0]): item[0] for item in work}.values())
        k_of = {id(j): rows(k_ref, j) for j in chunks}
        v_of = {id(j): [jnp.where(sel, v, jnp.zeros_like(v)) for sel in head_lanes]
                for j in chunks for v in [rows(v_ref, j)]}
        chains = [(j, half, hh, diag) for j, half, diag in work for hh in heads]
        zs = [_dot_nt(q_heads[half][hh], k_of[id(j)]) for j, half, hh, _ in chains]
        zs = [jnp.where(strict, z, -1e30) if diag else z for z, (_, _, _, diag) in zip(zs, chains)]
        sps = [jnp.maximum(z, 0.0) + jnp.log(1.0 + jnp.exp2(jnp.abs(z) * NEG_LOG2E)) for z in zs]
        later = [_dot(sp.astype(BF16), tri) for sp in sps]
        new_state = list(state)
        for c, (j, half, hh, _) in enumerate(chains):
            carries, acc = new_state[half]
            att = jnp.exp((zs[c] - sps[c]) + later[c] + carries[hh]).astype(BF16)
            chunk_sum = later[c][:, 0:1] - sps[c][:, 0:1]
            carries = tuple(carries[x] + chunk_sum if x == hh else carries[x] for x in heads)
            new_state[half] = (carries, acc + _dot(att, v_of[id(j)][hh]))
        return tuple(new_state)

    def max_carry(state):
        return functools.reduce(jnp.maximum, [jnp.max(c) for carries, _ in state for c in carries])

    zero_carry = jnp.zeros((SB_CHUNK, 1), F32)
    state = (((zero_carry, zero_carry), jnp.zeros((SB_CHUNK, LANES), F32)),) * SB_HALVES
    diag1, diag0 = SB_HALVES * i + 1, SB_HALVES * i
    state = advance(state, ((diag1, 1, True), (diag0, 0, True), (diag0, 1, False)))

    def live(loop):
        jj, _, carry_bound = loop
        return jnp.logical_and(jj < SB_HALVES * i, carry_bound > SB_DEAD_CARRY)

    def step(loop):
        jj, state, _ = loop
        j = SB_HALVES * i - 1 - jj
        state = advance(state, ((j, 0, False), (j, 1, False)))
        return jj + 1, state, max_carry(state)

    _, state, _ = lax.while_loop(live, step, (jnp.int32(0), state, max_carry(state)))
    for half in range(SB_HALVES):
        o_ref[0, half * SB_CHUNK:(half + 1) * SB_CHUNK, :] = state[half][1].astype(o_ref.dtype)


def _stick_breaking_attention(qkv, batch, seq):
    n_pairs = N_HEADS // 2
    view = qkv.reshape(batch, seq, 3 * D_ATTN)
    q_rows = SB_HALVES * SB_CHUNK
    j = np.arange(SB_CHUNK)
    tri = -(j[:, None] > j[None, :]).astype(np.float32)
    out = pl.pallas_call(
        _sb_attn_body,
        grid=(batch, n_pairs, seq // q_rows),
        in_specs=[pl.BlockSpec((1, q_rows, LANES), lambda b, p, i: (b, i, p)),
                  pl.BlockSpec((1, seq, LANES), lambda b, p, i: (b, 0, n_pairs + p)),
                  pl.BlockSpec((1, seq, LANES), lambda b, p, i: (b, 0, 2 * n_pairs + p)),
                  pl.BlockSpec((SB_CHUNK, SB_CHUNK), lambda b, p, i: (0, 0))],
        out_specs=pl.BlockSpec((1, q_rows, LANES), lambda b, p, i: (b, i, p)),
        out_shape=jax.ShapeDtypeStruct((batch, seq, D_ATTN), BF16),
        compiler_params=_params(3),
        name="stick_breaking_attn",
    )(view, view, view, jnp.asarray(tri, BF16))
    return out.reshape(batch * seq, D_ATTN)


def _out_proj_router_body(x_ref, w_ref, r_ref, g_ref, wh_ref, wl_ref, out_ref, hn_ref, gate_ref, idx_ref):
    h_new = r_ref[...] + _dot(x_ref[...], w_ref[...])
    out_ref[...] = h_new
    y = _rmsnorm_rows(h_new, g_ref[...])
    _to_row_tiles(y, hn_ref)
    y_hi, y_lo = _split_bf16(y)
    w_hi, w_lo = wh_ref[...], wl_ref[...]
    logits = _dot(y_hi, w_hi) + _dot(y_hi, w_lo) + _dot(y_lo, w_hi)
    lane = lax.broadcasted_iota(jnp.int32, logits.shape, 1)
    neg_inf = F32(-jnp.inf)
    logits = jnp.where(lane < N_EXPERTS, logits, neg_inf)
    v1 = jnp.max(logits, axis=1, keepdims=True)
    i1 = jnp.min(jnp.where(logits == v1, lane, LANES), axis=1, keepdims=True)
    rest = jnp.where(lane == i1, neg_inf, logits)
    v2 = jnp.max(rest, axis=1, keepdims=True)
    i2 = jnp.min(jnp.where(rest == v2, lane, LANES), axis=1, keepdims=True)
    e2 = jnp.exp(v2 - v1)
    w1 = 1.0 / (1.0 + e2)
    w2 = e2 * w1
    gate_ref[...] = jnp.where(lane == 0, w1, jnp.where(lane == 1, w2, 0.0))
    idx_ref[...] = jnp.where(lane == 0, i1, jnp.where(lane == 1, i2, 0))


def _out_proj_router(x, w_out, h, g, w_router):
    t, d = h.shape
    k = x.shape[1]
    tm = TM_EPILOGUE
    w_pad = jnp.zeros((d, LANES), F32).at[:, :N_EXPERTS].set(w_router)
    w_hi = w_pad.astype(BF16)
    w_lo = (w_pad - w_hi.astype(F32)).astype(BF16)
    row = lambda w: pl.BlockSpec((tm, w), lambda i: (i, 0))
    full = lambda r, c: pl.BlockSpec((r, c), lambda i: (0, 0))
    return pl.pallas_call(
        _out_proj_router_body,
        grid=(t // tm,),
        in_specs=[row(k), full(k, d), row(d), full(1, d), full(d, LANES), full(d, LANES)],
        out_specs=[row(d), pl.BlockSpec((tm * ROW_TILE, LANES), lambda i: (i, 0)), row(LANES), row(LANES)],
        out_shape=[jax.ShapeDtypeStruct((t, d), F32), jax.ShapeDtypeStruct((t * ROW_TILE, LANES), F32),
                   jax.ShapeDtypeStruct((t, LANES), F32), jax.ShapeDtypeStruct((t, LANES), jnp.int32)],
        compiler_params=_params(1),
        name="sb_out_proj_router",
    )(x, w_out, h, g.reshape(1, d), w_hi, w_lo)


def _route_layout(idx, n_slots):
    n_tiles = n_slots // TM_GROUP
    flat = idx[:, :TOP_K].reshape(-1)
    onehot = (flat[:, None] == jnp.arange(N_EXPERTS, dtype=jnp.int32)[None, :]).astype(jnp.int32)
    csum = jnp.cumsum(onehot, axis=0)
    rank = jnp.sum(csum * onehot, axis=1) - 1
    counts = csum[-1]
    padded = ((counts + TM_GROUP - 1) // TM_GROUP) * TM_GROUP
    pad_end = jnp.cumsum(padded)
    group_start = pad_end - padded
    dest = group_start[flat] + rank
    pad_count = padded - counts
    pad_cum = jnp.cumsum(pad_count)
    k = jnp.arange(n_slots - flat.shape[0], dtype=jnp.int32)
    e_k = jnp.minimum(jnp.sum((k[:, None] >= pad_cum[None, :]).astype(jnp.int32), axis=1), N_EXPERTS - 1)
    in_group = (group_start + counts)[e_k] + k - (pad_cum - pad_count)[e_k]
    pad_slots = jnp.where(k < pad_cum[-1], in_group, pad_end[-1] + k - pad_cum[-1])
    tile_start = jnp.arange(n_tiles, dtype=jnp.int32) * TM_GROUP
    tile_expert = jnp.minimum(jnp.sum((tile_start[:, None] >= pad_end[None, :]).astype(jnp.int32), axis=1),
                              N_EXPERTS - 1)
    n_used = (pad_end[-1:] // TM_GROUP).astype(jnp.int32)
    return dest.reshape(-1, TOP_K), pad_slots, tile_expert, n_used


def _to_row_tiles(rows, out_ref):
    n, d = rows.shape
    for c in range(d // LANES):
        out_ref[pl.ds(c, n, stride=d // LANES), :] = rows[:, c * LANES:(c + 1) * LANES]


def _from_row_tiles(tiles_ref, n):
    return jnp.concatenate([tiles_ref[pl.ds(c, n, stride=ROW_TILE), :] for c in range(ROW_TILE)], axis=1)


def _row_copies(wait, idx_ref, n_rows, src_hbm, dst_for_row, sem):
    def copy(r):
        src = src_hbm.at[pl.ds(pl.multiple_of(idx_ref[0, 0, r] * ROW_TILE, ROW_TILE), ROW_TILE)]
        return pltpu.make_async_copy(src, dst_for_row(r), sem)

    def trip(g, c):
        for u in range(ROW_DMA_UNROLL):
            if wait:
                copy(g * ROW_DMA_UNROLL + u).wait()
            else:
                copy(g * ROW_DMA_UNROLL + u).start(priority=u % 2)
        return c

    lax.fori_loop(0, n_rows // ROW_DMA_UNROLL, trip, 0)


def _prefetched_row_gather(idx_ref, next_idx_ref, n_rows, src_hbm, dst_for_row, sems):
    i = pl.program_id(0)
    cur = i & 1

    @pl.when(i == 0)
    def _():
        _row_copies(False, idx_ref, n_rows, src_hbm, functools.partial(dst_for_row, 0), sems.at[0])

    @pl.when(i + 1 < pl.num_programs(0))
    def _():
        _row_copies(False, next_idx_ref, n_rows, src_hbm, functools.partial(dst_for_row, 1 - cur), sems.at[1 - cur])

    _row_copies(True, idx_ref, n_rows, src_hbm, functools.partial(dst_for_row, cur), sems.at[cur])
    return cur


def _tile_rows(r):
    return pl.ds(pl.multiple_of(r * ROW_TILE, ROW_TILE), ROW_TILE)


def _dispatch_body(dest_ref, x_ref, xs_hbm, sem):
    def copy(e):
        src = x_ref.at[_tile_rows(lax.shift_right_logical(e, 1))]
        return pltpu.make_async_copy(src, xs_hbm.at[_tile_rows(dest_ref[0, 0, e])], sem)

    def trip(wait, g, c):
        for u in range(ROW_DMA_UNROLL):
            if wait:
                copy(g * ROW_DMA_UNROLL + u).wait()
            else:
                copy(g * ROW_DMA_UNROLL + u).start(priority=u % 2)
        return c

    n_trips = dest_ref.shape[2] // ROW_DMA_UNROLL
    lax.fori_loop(0, n_trips, functools.partial(trip, False), 0)
    lax.fori_loop(0, n_trips, functools.partial(trip, True), 0)


def _dispatch_rows(x, dest, pad_dest):
    assert TOP_K == 2
    tm = TM_COMBINE
    n_tokens = x.shape[0] // ROW_TILE
    token_steps = n_tokens // tm
    all_dest = jnp.concatenate([dest.reshape(-1), pad_dest]).reshape(-1, 1, TOP_K * tm)
    n_slots = all_dest.size
    return pl.pallas_call(
        _dispatch_body,
        grid=(all_dest.shape[0],),
        in_specs=[pl.BlockSpec((1, 1, TOP_K * tm), lambda i: (i, 0, 0), memory_space=pltpu.SMEM),
                  pl.BlockSpec((tm * ROW_TILE, LANES), lambda i: (jnp.minimum(i, token_steps - 1), 0))],
        out_specs=pl.BlockSpec(memory_space=pl.ANY),
        out_shape=jax.ShapeDtypeStruct((n_slots * ROW_TILE, LANES), F32),
        scratch_shapes=[pltpu.SemaphoreType.DMA(())],
        compiler_params=_params(1),
        name="moe_dispatch",
    )(all_dest, x)


def _stage_expert_weight(te_ref, w_hbm, layer, w_bf, stage, sems, chunk_of):
    t = pl.program_id(0)
    expert = te_ref[t]
    n_chunks = w_bf.size // stage[0].size
    n_bufs = stage.shape[0]

    @pl.when(jnp.logical_or(t == 0, expert != te_ref[jnp.maximum(t - 1, 0)]))
    def _():
        def copy(c):
            buf = c % n_bufs
            return pltpu.make_async_copy(chunk_of(w_hbm.at[layer, expert], c), stage.at[buf], sems.at[buf])

        for c in range(min(n_bufs - 1, n_chunks)):
            copy(c).start()
        for c in range(n_chunks):
            copy(c).wait()
            chunk_of(w_bf, c)[...] = stage[c % n_bufs].astype(w_bf.dtype)
            if c + n_bufs - 1 < n_chunks:
                copy(c + n_bufs - 1).start()


def _group_swiglu_body(te_ref, nu_ref, x_ref, w_hbm, o_ref, w_bf, stage, sems, *, layer):
    used = pl.program_id(0) < nu_ref[0]

    @pl.when(used)
    def _():
        _stage_expert_weight(te_ref, w_hbm, layer, w_bf, stage, sems,
                             lambda ref, c: ref.at[:, pl.ds(c * TN_FF, TN_FF)])
        _swiglu_cols(_from_row_tiles(x_ref, TM_GROUP).astype(BF16), w_bf, o_ref)

    @pl.when(jnp.logical_not(used))
    def _():
        o_ref[...] = jnp.zeros_like(o_ref)


def _group_down_body(te_ref, nu_ref, x_ref, w_hbm, o_ref, w_bf, stage, sems, *, layer):
    used = pl.program_id(0) < nu_ref[0]

    @pl.when(used)
    def _():
        rows = stage.shape[1]
        _stage_expert_weight(te_ref, w_hbm, layer, w_bf, stage, sems,
                             lambda ref, c: ref.at[pl.ds(c * rows, rows), :])
        _to_row_tiles(_dot(x_ref[...], w_bf[...]), o_ref)

    @pl.when(jnp.logical_not(used))
    def _():
        o_ref[...] = jnp.zeros_like(o_ref)


def _group_swiglu_up(xs, w_gate_up, layer, tile_expert, n_used):
    n_slots = xs.shape[0] // ROW_TILE
    k = ROW_TILE * LANES
    n_tiles = n_slots // TM_GROUP
    last = lambda t, nu: jnp.minimum(t, nu[0] - 1)
    grid_spec = pltpu.PrefetchScalarGridSpec(
        num_scalar_prefetch=2,
        grid=(n_tiles,),
        in_specs=[pl.BlockSpec((TM_GROUP * ROW_TILE, LANES), lambda t, te, nu: (last(t, nu), 0)),
                  pl.BlockSpec(memory_space=pl.ANY)],
        out_specs=pl.BlockSpec((TM_GROUP, D_FF), lambda t, te, nu: (t, 0)),
        scratch_shapes=[pltpu.VMEM((k, 2 * D_FF), BF16), pltpu.VMEM((WEIGHT_STAGE_BUFFERS, k, TN_FF), F32),
                        pltpu.SemaphoreType.DMA((WEIGHT_STAGE_BUFFERS,))],
    )
    return pl.pallas_call(
        functools.partial(_group_swiglu_body, layer=layer),
        grid_spec=grid_spec,
        out_shape=jax.ShapeDtypeStruct((n_slots, D_FF), BF16),
        compiler_params=_params(1),
        name="moe_swiglu_up",
    )(tile_expert, n_used, xs, w_gate_up)


def _group_down(act, w_down, layer, tile_expert, n_used):
    n_slots, k = act.shape
    n = w_down.shape[-1]
    n_tiles = n_slots // TM_GROUP
    last = lambda t, nu: jnp.minimum(t, nu[0] - 1)
    grid_spec = pltpu.PrefetchScalarGridSpec(
        num_scalar_prefetch=2,
        grid=(n_tiles,),
        in_specs=[pl.BlockSpec((TM_GROUP, k), lambda t, te, nu: (last(t, nu), 0)),
                  pl.BlockSpec(memory_space=pl.ANY)],
        out_specs=pl.BlockSpec((TM_GROUP * ROW_TILE, LANES), lambda t, te, nu: (t, 0)),
        scratch_shapes=[pltpu.VMEM((k, n), BF16), pltpu.VMEM((WEIGHT_STAGE_BUFFERS, TN_FF, n), F32),
                        pltpu.SemaphoreType.DMA((WEIGHT_STAGE_BUFFERS,))],
    )
    return pl.pallas_call(
        functools.partial(_group_down_body, layer=layer),
        grid_spec=grid_spec,
        out_shape=jax.ShapeDtypeStruct((n_slots * ROW_TILE, LANES), F32),
        compiler_params=_params(1),
        name="moe_down",
    )(tile_expert, n_used, act, w_down)


def _combined_rows(slot_ref, next_slot_ref, gate_ref, h_ref, y_hbm, buf, sems):
    cur = _prefetched_row_gather(
        slot_ref, next_slot_ref, TOP_K * TM_COMBINE, y_hbm,
        lambda b, e: buf.at[b, e & 1, _tile_rows(lax.shift_right_logical(e, 1))], sems)
    gates = gate_ref[...]
    best, second = [_from_row_tiles(buf.at[cur, c], TM_COMBINE) for c in range(TOP_K)]
    return h_ref[...] + gates[:, 0:1] * best + gates[:, 1:2] * second


def _moe_combine_body(slot_ref, next_slot_ref, gate_ref, h_ref, y_hbm, out_ref, buf, sems):
    out_ref[...] = _combined_rows(slot_ref, next_slot_ref, gate_ref, h_ref, y_hbm, buf, sems)


def _moe_combine_norm_body(slot_ref, next_slot_ref, gate_ref, h_ref, g_ref, y_hbm, out_ref, buf, sems):
    rows = _combined_rows(slot_ref, next_slot_ref, gate_ref, h_ref, y_hbm, buf, sems)
    out_ref[...] = _rmsnorm_rows(rows, g_ref[...])


def _moe_combine(h, y_slots, slots, gates, g_final=None):
    t, d = h.shape
    tm = TM_COMBINE
    n_tiles = t // tm
    slot_idx = slots.reshape(n_tiles, 1, TOP_K * tm)
    in_specs = [pl.BlockSpec((1, 1, TOP_K * tm), lambda i: (i, 0, 0), memory_space=pltpu.SMEM),
                pl.BlockSpec((1, 1, TOP_K * tm), lambda i: (jnp.minimum(i + 1, n_tiles - 1), 0, 0),
                             memory_space=pltpu.SMEM),
                pl.BlockSpec((tm, LANES), lambda i: (i, 0)),
                pl.BlockSpec((tm, d), lambda i: (i, 0))]
    args = [slot_idx, slot_idx, gates, h]
    if g_final is not None:
        in_specs.append(pl.BlockSpec((1, d), lambda i: (0, 0)))
        args.append(g_final.reshape(1, d))
    return pl.pallas_call(
        _moe_combine_body if g_final is None else _moe_combine_norm_body,
        grid=(n_tiles,),
        in_specs=in_specs + [pl.BlockSpec(memory_space=pl.ANY)],
        out_specs=pl.BlockSpec((tm, d), lambda i: (i, 0)),
        out_shape=jax.ShapeDtypeStruct((t, d), F32),
        scratch_shapes=[pltpu.VMEM((2, TOP_K, tm * ROW_TILE, LANES), F32), pltpu.SemaphoreType.DMA((2,))],
        compiler_params=_params(1),
        name="moe_combine",
    )(*args, y_slots)


def _moe_layer(h, hn, gates, idx, w_gate_up, w_down, layer, g_final=None):
    t = h.shape[0]
    n_slots = t * TOP_K + N_EXPERTS * TM_GROUP
    slots, pad_slots, tile_expert, n_used = _route_layout(idx, n_slots)
    xs = _dispatch_rows(hn, slots, pad_slots)
    act = _group_swiglu_up(xs, w_gate_up, layer, tile_expert, n_used)
    y_slots = _group_down(act, w_down, layer, tile_expert, n_used)
    return _moe_combine(h, y_slots, slots, gates, g_final)


def kernel(x, mix_norm, ffn_norm, dil_w_in, dil_w_out, sb_w_qkv, sb_w_out, ffn_w_gate_up, ffn_w_down,
           moe_w_router, moe_w_gate_up, moe_w_down, final_norm):
    batch, seq, d = x.shape
    depth = mix_norm.shape[0]
    h = x.reshape(batch * seq, d)
    hn = None
    for i in range(depth):
        j = i // 2
        last = i + 1 == depth
        if i % 2 == 0:
            h, hn = _dilated_mixer(h, mix_norm[i], dil_w_in[j], dil_w_out[j], ffn_norm[i], batch, seq)
            act = _swiglu_up(hn, ffn_w_gate_up[j].astype(BF16))
            if last:
                h, hn = _matmul_residual(act, ffn_w_down[j].astype(BF16), h, "ffn_down"), None
            else:
                h, hn = _matmul_residual(act, ffn_w_down[j].astype(BF16), h, "ffn_down", g_next=mix_norm[i + 1])
        else:
            if hn is None:
                hn = _rmsnorm(h, mix_norm[i], BF16)
            qkv = _matmul(hn, sb_w_qkv[j].astype(BF16), BF16, "sb_qkv_proj")
            y = _stick_breaking_attention(qkv, batch, seq)
            h, hn_tiles, gates, idx = _out_proj_router(y, sb_w_out[j].astype(BF16), h, ffn_norm[i], moe_w_router[j])
            h = _moe_layer(h, hn_tiles, gates, idx, moe_w_gate_up, moe_w_down, j,
                           g_final=final_norm if last else None)
            hn = None
            if last:
                return h.reshape(batch, seq, d)
    return _rmsnorm(h, final_norm, F32).reshape(batch, seq, d)
```

```python
import functools

import numpy as np
import jax
import jax.numpy as jnp
from jax import lax
from jax.experimental import pallas as pl
from jax.experimental.pallas import tpu as pltpu

D_MODEL = 1024
N_HEADS = 16
HEAD_DIM = 64
D_ATTN = N_HEADS * HEAD_DIM
DILATED_GROUPS = ((128, 1), (512, 4), (2048, 16))
N_GROUPS = len(DILATED_GROUPS)
BLOCK = 128
D_FF = 3584
N_EXPERTS = 8
TOP_K = 2
RMS_EPS = 1e-6
ATTN_SCALE = 0.125
NEG_LOG2E = -float(np.float32(np.log2(np.e)))

LANES = 128
VMEM_LIMIT_BYTES = 56 * 1024 * 1024

TM = 1024
TN = 1024
TN_FF = 512
TM_GROUP = 512
TM_COMBINE = 256
TM_EPILOGUE = 256
SB_CHUNK = 256
SB_HALVES = 2
SB_DEAD_CARRY = -104.0
DIL_PAIRS_PER_STAGE = 4
ROW_DMA_UNROLL = 8
ROW_TILE = D_MODEL // LANES
WEIGHT_STAGE_BUFFERS = 4

F32 = jnp.float32
BF16 = jnp.bfloat16

ALIBI_SLOPES = tuple(float(np.float32(2.0 ** (-8.0 * (h + 1) / N_HEADS))) for h in range(N_HEADS))


def _params(n_axes):
    return pltpu.CompilerParams(dimension_semantics=("arbitrary",) * n_axes, vmem_limit_bytes=VMEM_LIMIT_BYTES)


def _dot(a, b):
    return jnp.dot(a, b, preferred_element_type=F32)


def _dot_nt(a, b):
    return lax.dot_general(a, b, (((1,), (1,)), ((), ())), preferred_element_type=F32)


def _split_bf16(x):
    hi = x.astype(BF16)
    lo = (x - hi.astype(F32)).astype(BF16)
    return hi, lo


def _rmsnorm_rows(x, g):
    return x * lax.rsqrt(jnp.mean(x * x, axis=-1, keepdims=True) + RMS_EPS) * g


def _rmsnorm_body(x_ref, g_ref, o_ref):
    o_ref[...] = _rmsnorm_rows(x_ref[...], g_ref[...]).astype(o_ref.dtype)


def _rmsnorm(x, g, out_dtype):
    t, d = x.shape
    return pl.pallas_call(
        _rmsnorm_body,
        grid=(t // TM,),
        in_specs=[pl.BlockSpec((TM, d), lambda i: (i, 0)), pl.BlockSpec((1, d), lambda i: (0, 0))],
        out_specs=pl.BlockSpec((TM, d), lambda i: (i, 0)),
        out_shape=jax.ShapeDtypeStruct((t, d), out_dtype),
        compiler_params=_params(1),
        name="rmsnorm",
    )(x, g.reshape(1, d))


def _rmsnorm_residue_body(x_ref, g_ref, *refs):
    o_refs, lanes_ref = refs[:-1], refs[-1]
    y = _rmsnorm_rows(x_ref[0], g_ref[...])
    n_chunks = y.shape[1] // LANES
    for c in range(n_chunks):
        lanes_ref[c] = y[:, c * LANES:(c + 1) * LANES]
    for o_ref, (_, dilation) in zip(o_refs, DILATED_GROUPS):
        if dilation == 1:
            o_ref[0, 0] = y.astype(o_ref.dtype)
            continue
        rows = TM // dilation
        for r in range(dilation):
            for c in range(n_chunks):
                picked = lanes_ref[c, pl.ds(r, rows, stride=dilation), :]
                o_ref[0, r, :, c * LANES:(c + 1) * LANES] = picked.astype(o_ref.dtype)


def _rmsnorm_by_residue(x, g, batch, seq):
    d_model = x.shape[1]
    tiles = seq // TM
    out_specs, out_shape = [], []
    for _, dilation in DILATED_GROUPS:
        out_specs.append(pl.BlockSpec((1, dilation, TM // dilation, d_model), lambda b, i: (b, 0, i, 0)))
        out_shape.append(jax.ShapeDtypeStruct((batch, dilation, seq // dilation, d_model), BF16))
    outs = pl.pallas_call(
        _rmsnorm_residue_body,
        grid=(batch, tiles),
        in_specs=[pl.BlockSpec((1, TM, d_model), lambda b, i: (b, i, 0)),
                  pl.BlockSpec((1, d_model), lambda b, i: (0, 0))],
        out_specs=out_specs,
        out_shape=out_shape,
        scratch_shapes=[pltpu.VMEM((d_model // LANES, TM, LANES), F32)],
        compiler_params=_params(2),
        name="rmsnorm_by_residue",
    )(x.reshape(batch, seq, d_model), g.reshape(1, d_model))
    return [o.reshape(batch * seq, d_model) for o in outs]


def _mm_body(x_ref, w_ref, o_ref):
    x = x_ref[...]
    for c in range(0, o_ref.shape[1], TN):
        o_ref[:, c:c + TN] = _dot(x, w_ref[:, c:c + TN]).astype(o_ref.dtype)


def _mm_res_body(x_ref, w_ref, r_ref, o_ref):
    o_ref[...] = r_ref[...] + _dot(x_ref[...], w_ref[...])


def _mm_res_norm_body(x_ref, w_ref, r_ref, g_ref, o_ref, hn_ref):
    h_new = r_ref[...] + _dot(x_ref[...], w_ref[...])
    o_ref[...] = h_new
    hn_ref[...] = _rmsnorm_rows(h_new, g_ref[...]).astype(hn_ref.dtype)


def _matmul(x, w, out_dtype, name, first_col=0, n_cols=None):
    m, k = x.shape
    n = w.shape[1] if n_cols is None else n_cols
    panel = first_col // n
    return pl.pallas_call(
        _mm_body,
        grid=(m // TM,),
        in_specs=[pl.BlockSpec((TM, k), lambda i: (i, 0)), pl.BlockSpec((k, n), lambda i: (0, panel))],
        out_specs=pl.BlockSpec((TM, n), lambda i: (i, 0)),
        out_shape=jax.ShapeDtypeStruct((m, n), out_dtype),
        compiler_params=_params(1),
        name=name,
    )(x, w)


def _matmul_residual(x, w, res, name, g_next=None):
    m, k = x.shape
    n = w.shape[1]
    row = pl.BlockSpec((TM, n), lambda i: (i, 0))
    in_specs = [pl.BlockSpec((TM, k), lambda i: (i, 0)), pl.BlockSpec((k, n), lambda i: (0, 0)), row]
    if g_next is None:
        return pl.pallas_call(
            _mm_res_body,
            grid=(m // TM,),
            in_specs=in_specs,
            out_specs=row,
            out_shape=jax.ShapeDtypeStruct((m, n), F32),
            compiler_params=_params(1),
            name=name,
        )(x, w, res)
    return pl.pallas_call(
        _mm_res_norm_body,
        grid=(m // TM,),
        in_specs=in_specs + [pl.BlockSpec((1, n), lambda i: (0, 0))],
        out_specs=[row, row],
        out_shape=[jax.ShapeDtypeStruct((m, n), F32), jax.ShapeDtypeStruct((m, n), BF16)],
        compiler_params=_params(1),
        name=name,
    )(x, w, res, g_next.reshape(1, n))


def _swiglu(g, u):
    return g * (1.0 / (1.0 + jnp.exp(-g))) * u


def _swiglu_cols(x, w_gate_up, o_ref):
    for c in range(0, D_FF, TN_FF):
        gate = _dot(x, w_gate_up[:, c:c + TN_FF])
        up = _dot(x, w_gate_up[:, D_FF + c:D_FF + c + TN_FF])
        o_ref[:, c:c + TN_FF] = _swiglu(gate, up).astype(o_ref.dtype)


def _swiglu_body(x_ref, w_ref, o_ref):
    _swiglu_cols(x_ref[...], w_ref, o_ref)


def _swiglu_up(x, w_gate_up):
    m, k = x.shape
    return pl.pallas_call(
        _swiglu_body,
        grid=(m // TM,),
        in_specs=[pl.BlockSpec((TM, k), lambda i: (i, 0)),
                  pl.BlockSpec((k, 2 * D_FF), lambda i: (0, 0))],
        out_specs=pl.BlockSpec((TM, D_FF), lambda i: (i, 0)),
        out_shape=jax.ShapeDtypeStruct((m, D_FF), BF16),
        compiler_params=_params(1),
        name="swiglu_up",
    )(x, w_gate_up)


def _alibi_window_bias(dilation):
    a = np.arange(BLOCK)[:, None]
    m = np.arange(2 * BLOCK)[None, :]
    rel = BLOCK + a - m
    in_window = (rel >= 0) & (rel <= BLOCK)
    slopes = np.asarray(ALIBI_SLOPES, np.float32)[:, None, None]
    bias = -slopes * (rel * dilation).astype(np.float32)[None]
    later = np.where(in_window[None], bias, -np.inf)
    first = np.where((m >= BLOCK)[None], later, -np.inf)
    return np.stack([first, later]).astype(np.float32)


def _dil_attn_body(q_ref, kc_ref, kp_ref, vc_ref, vp_ref, bias_ref, o_ref, lse_ref):
    n_keys = 2 * BLOCK
    lane = lax.broadcasted_iota(jnp.int32, (BLOCK, LANES), 1)
    first_head = lane < HEAD_DIM
    first_head_keys = lax.broadcasted_iota(jnp.int32, (n_keys, LANES), 1) < HEAD_DIM
    lse_tile = jnp.zeros((BLOCK, LANES), F32)

    def by_head(pair_rows):
        zero = jnp.zeros_like(pair_rows)
        return jnp.concatenate([jnp.where(first_head_keys, pair_rows, zero),
                                jnp.where(first_head_keys, zero, pair_rows)], axis=0)

    for first_pair in range(0, N_HEADS // 2, DIL_PAIRS_PER_STAGE):
        pairs = range(first_pair, first_pair + DIL_PAIRS_PER_STAGE)
        cols = {p: slice(p * LANES, (p + 1) * LANES) for p in pairs}
        scores = {p: _dot_nt(q_ref[:, cols[p]] * ATTN_SCALE,
                             by_head(jnp.concatenate([kp_ref[:, cols[p]], kc_ref[:, cols[p]]], axis=0)))
                  for p in pairs}
        probs, norms = {}, {}
        for p in pairs:
            pair_probs = []
            for hh in range(2):
                h = 2 * p + hh
                s = scores[p][:, hh * n_keys:(hh + 1) * n_keys] + bias_ref[0, h]
                s_max = jnp.max(s, axis=1, keepdims=True)
                e = jnp.exp(s - s_max)
                z = jnp.sum(e, axis=1, keepdims=True)
                lse_tile = jnp.where(lane == h, s_max + jnp.log(z), lse_tile)
                pair_probs.append(e.astype(BF16))
                norms[h] = z
            probs[p] = jnp.concatenate(pair_probs, axis=1)
        for p in pairs:
            v_heads = by_head(jnp.concatenate([vp_ref[:, cols[p]], vc_ref[:, cols[p]]], axis=0))
            o_pair = _dot(probs[p], v_heads) / jnp.where(first_head, norms[2 * p], norms[2 * p + 1])
            o_ref[:, cols[p]] = o_pair.astype(o_ref.dtype)
    lse_ref[...] = lse_tile


def _dilated_attention(proj, batch, seq, group):
    _, dilation = DILATED_GROUPS[group]
    n_blocks = seq // dilation // BLOCK

    def block(which, back):
        return lambda b, r, n: ((b * dilation + r) * n_blocks + jnp.maximum(n - back, 0), which)

    blk = (BLOCK, D_ATTN)
    bias = jnp.asarray(_alibi_window_bias(dilation))
    return pl.pallas_call(
        _dil_attn_body,
        grid=(batch, dilation, n_blocks),
        in_specs=[pl.BlockSpec(blk, block(0, 0)), pl.BlockSpec(blk, block(1, 0)), pl.BlockSpec(blk, block(1, 1)),
                  pl.BlockSpec(blk, block(2, 0)), pl.BlockSpec(blk, block(2, 1)),
                  pl.BlockSpec((1,) + bias.shape[1:], lambda b, r, n: (jnp.minimum(n, 1), 0, 0, 0))],
        out_specs=[pl.BlockSpec(blk, block(0, 0)), pl.BlockSpec((BLOCK, LANES), block(0, 0))],
        out_shape=[jax.ShapeDtypeStruct((batch * seq, D_ATTN), BF16),
                   jax.ShapeDtypeStruct((batch * seq, LANES), F32)],
        compiler_params=_params(3),
        name=f"dilated_attn_g{group}",
    )(proj, proj, proj, proj, proj, bias)


def _dil_out_body(o0_ref, o1_ref, o2_ref, l0_ref, l1_ref, l2_ref, e_ref, w_ref, h_ref, g_ref, out_ref, hn_ref,
                  rows_ref, stat_ref):
    tile = h_ref.shape[1]
    n_chunks = D_ATTN // LANES
    outs, lses = [], []
    for grp, (o_ref, l_ref) in enumerate(zip((o0_ref, o1_ref, o2_ref), (l0_ref, l1_ref, l2_ref))):
        dilation = DILATED_GROUPS[grp][1]
        if dilation == 1:
            outs.append(o_ref[0, 0].astype(F32))
            lses.append(l_ref[0, 0])
            continue
        for r in range(dilation):
            rows = pl.ds(r, tile // dilation, stride=dilation)
            stat_ref[grp, rows, :] = l_ref[0, r]
            for c in range(n_chunks):
                rows_ref[grp, c, rows, :] = o_ref[0, r, :, c * LANES:(c + 1) * LANES].astype(F32)
        outs.append(jnp.concatenate([rows_ref[grp, c] for c in range(n_chunks)], axis=1))
        lses.append(stat_ref[grp])
    top = jnp.maximum(jnp.maximum(lses[0], lses[1]), lses[2])
    es = [jnp.exp(l - top) for l in lses]
    den = es[0] + es[1] + es[2]
    expand = e_ref[...]
    y = None
    for grp in range(N_GROUPS):
        term = _dot((es[grp] / den).astype(BF16), expand) * outs[grp]
        y = term if y is None else y + term
    h_new = h_ref[0] + _dot(y.astype(BF16), w_ref[...])
    out_ref[0] = h_new
    hn_ref[0] = _rmsnorm_rows(h_new, g_ref[...]).astype(hn_ref.dtype)


def _dilated_out_proj(outs, lses, w_out, h, g_next, batch, seq):
    tm = TM_EPILOGUE
    expand = (np.arange(LANES)[:, None] == (np.arange(D_ATTN)[None, :] // HEAD_DIM)).astype(np.float32)

    def by_residue(width, dilation):
        return pl.BlockSpec((1, dilation, tm // dilation, width), lambda b, i: (b, 0, i, 0))

    def grouped(arrays, width):
        return [a.reshape(batch, d, seq // d, width) for a, (_, d) in zip(arrays, DILATED_GROUPS)]

    row = pl.BlockSpec((1, tm, D_MODEL), lambda b, i: (b, i, 0))
    full = lambda r, c: pl.BlockSpec((r, c), lambda b, i: (0, 0))
    h_new, hn = pl.pallas_call(
        _dil_out_body,
        grid=(batch, seq // tm),
        in_specs=[by_residue(D_ATTN, d) for _, d in DILATED_GROUPS]
        + [by_residue(LANES, d) for _, d in DILATED_GROUPS]
        + [full(LANES, D_ATTN), full(D_ATTN, D_MODEL), row, full(1, D_MODEL)],
        out_specs=[row, row],
        out_shape=[jax.ShapeDtypeStruct((batch, seq, D_MODEL), F32),
                   jax.ShapeDtypeStruct((batch, seq, D_MODEL), BF16)],
        scratch_shapes=[pltpu.VMEM((N_GROUPS, D_ATTN // LANES, tm, LANES), F32),
                        pltpu.VMEM((N_GROUPS, tm, LANES), F32)],
        compiler_params=_params(2),
        name="dilated_out_proj",
    )(*grouped(outs, D_ATTN), *grouped(lses, LANES), jnp.asarray(expand, BF16), w_out,
      h.reshape(batch, seq, D_MODEL), g_next.reshape(1, D_MODEL))
    return h_new.reshape(batch * seq, D_MODEL), hn.reshape(batch * seq, D_MODEL)


def _dilated_mixer(h, g, w_in, w_out, g_next, batch, seq):
    w_in = w_in.astype(BF16)
    parts = []
    for group, hn in enumerate(_rmsnorm_by_residue(h, g, batch, seq)):
        proj = _matmul(hn, w_in, BF16, f"dilated_in_proj_g{group}", first_col=group * 3 * D_ATTN, n_cols=3 * D_ATTN)
        parts.append(_dilated_attention(proj, batch, seq, group))
    return _dilated_out_proj([p[0] for p in parts], [p[1] for p in parts], w_out.astype(BF16), h, g_next,
                             batch, seq)


def _sb_attn_body(q_ref, k_ref, v_ref, tri_ref, o_ref):
    i = pl.program_id(2)
    row = lax.broadcasted_iota(jnp.int32, (SB_CHUNK, SB_CHUNK), 0)
    col = lax.broadcasted_iota(jnp.int32, (SB_CHUNK, SB_CHUNK), 1)
    strict = col < row
    first_head = lax.broadcasted_iota(jnp.int32, (SB_CHUNK, LANES), 1) < HEAD_DIM
    head_lanes = (first_head, ~first_head)
    tri = tri_ref[...]
    heads = range(2)
    q_heads = []
    for half in range(SB_HALVES):
        q_pair = q_ref[0, half * SB_CHUNK:(half + 1) * SB_CHUNK, :] * ATTN_SCALE
        q_heads.append([jnp.where(sel, q_pair, jnp.zeros_like(q_pair)) for sel in head_lanes])

    def advance(state, work):
        def rows(ref, j):
            return ref[0, pl.ds(pl.multiple_of(j * SB_CHUNK, SB_CHUNK), SB_CHUNK), :]

        chunks = list({id(item[0]): item[0] for item in work}.values())
        k_of = {id(j): rows(k_ref, j) for j in chunks}
        v_of = {id(j): [jnp.where(sel, v, jnp.zeros_like(v)) for sel in head_lanes]
                for j in chunks for v in [rows(v_ref, j)]}
        chains = [(j, half, hh, diag) for j, half, diag in work for hh in heads]
        zs = [_dot_nt(q_heads[half][hh], k_of[id(j)]) for j, half, hh, _ in chains]
        zs = [jnp.where(strict, z, -1e30) if diag else z for z, (_, _, _, diag) in zip(zs, chains)]
        sps = [jnp.maximum(z, 0.0) + jnp.log(1.0 + jnp.exp2(jnp.abs(z) * NEG_LOG2E)) for z in zs]
        later = [_dot(sp.astype(BF16), tri) for sp in sps]
        new_state = list(state)
        for c, (j, half, hh, _) in enumerate(chains):
            carries, acc = new_state[half]
            att = jnp.exp((zs[c] - sps[c]) + later[c] + carries[hh]).astype(BF16)
            chunk_sum = later[c][:, 0:1] - sps[c][:, 0:1]
            carries = tuple(carries[x] + chunk_sum if x == hh else carries[x] for x in heads)
            new_state[half] = (carries, acc + _dot(att, v_of[id(j)][hh]))
        return tuple(new_state)

    def max_carry(state, half):
        return functools.reduce(jnp.maximum, [jnp.max(c) for c in state[half][0]])

    zero_carry = jnp.zeros((SB_CHUNK, 1), F32)
    state = (((zero_carry, zero_carry), jnp.zeros((SB_CHUNK, LANES), F32)),) * SB_HALVES
    diag1, diag0 = SB_HALVES * i + 1, SB_HALVES * i
    state = advance(state, ((diag1, 1, True), (diag0, 0, True), (diag0, 1, False)))

    def walk(halves, watched, start):
        def live(loop):
            jj, _, carry_bound = loop
            return jnp.logical_and(jj < SB_HALVES * i, carry_bound > SB_DEAD_CARRY)

        def step(loop):
            jj, state, _ = loop
            j = SB_HALVES * i - 1 - jj
            state = advance(state, tuple((j, half, False) for half in halves))
            return jj + 1, state, max_carry(state, watched)

        jj, state = start
        jj, state, _ = lax.while_loop(live, step, (jj, state, max_carry(state, watched)))
        return jj, state

    _, state = walk((0,), 0, walk((0, 1), 1, (jnp.int32(0), state)))
    for half in range(SB_HALVES):
        o_ref[0, half * SB_CHUNK:(half + 1) * SB_CHUNK, :] = state[half][1].astype(o_ref.dtype)


def _stick_breaking_attention(qkv, batch, seq):
    n_pairs = N_HEADS // 2
    view = qkv.reshape(batch, seq, 3 * D_ATTN)
    q_rows = SB_HALVES * SB_CHUNK
    j = np.arange(SB_CHUNK)
    tri = -(j[:, None] > j[None, :]).astype(np.float32)
    out = pl.pallas_call(
        _sb_attn_body,
        grid=(batch, n_pairs, seq // q_rows),
        in_specs=[pl.BlockSpec((1, q_rows, LANES), lambda b, p, i: (b, i, p)),
                  pl.BlockSpec((1, seq, LANES), lambda b, p, i: (b, 0, n_pairs + p)),
                  pl.BlockSpec((1, seq, LANES), lambda b, p, i: (b, 0, 2 * n_pairs + p)),
                  pl.BlockSpec((SB_CHUNK, SB_CHUNK), lambda b, p, i: (0, 0))],
        out_specs=pl.BlockSpec((1, q_rows, LANES), lambda b, p, i: (b, i, p)),
        out_shape=jax.ShapeDtypeStruct((batch, seq, D_ATTN), BF16),
        compiler_params=_params(3),
        name="stick_breaking_attn",
    )(view, view, view, jnp.asarray(tri, BF16))
    return out.reshape(batch * seq, D_ATTN)


def _out_proj_router_body(x_ref, w_ref, r_ref, g_ref, wh_ref, wl_ref, out_ref, hn_ref, gate_ref, idx_ref):
    h_new = r_ref[...] + _dot(x_ref[...], w_ref[...])
    out_ref[...] = h_new
    y = _rmsnorm_rows(h_new, g_ref[...])
    _to_row_tiles(y, hn_ref)
    y_hi, y_lo = _split_bf16(y)
    w_hi, w_lo = wh_ref[...], wl_ref[...]
    logits = _dot(y_hi, w_hi) + _dot(y_hi, w_lo) + _dot(y_lo, w_hi)
    lane = lax.broadcasted_iota(jnp.int32, logits.shape, 1)
    neg_inf = F32(-jnp.inf)
    logits = jnp.where(lane < N_EXPERTS, logits, neg_inf)
    v1 = jnp.max(logits, axis=1, keepdims=True)
    i1 = jnp.min(jnp.where(logits == v1, lane, LANES), axis=1, keepdims=True)
    rest = jnp.where(lane == i1, neg_inf, logits)
    v2 = jnp.max(rest, axis=1, keepdims=True)
    i2 = jnp.min(jnp.where(rest == v2, lane, LANES), axis=1, keepdims=True)
    e2 = jnp.exp(v2 - v1)
    w1 = 1.0 / (1.0 + e2)
    w2 = e2 * w1
    gate_ref[...] = jnp.where(lane == 0, w1, jnp.where(lane == 1, w2, 0.0))
    idx_ref[...] = jnp.where(lane == 0, i1, jnp.where(lane == 1, i2, 0))


def _out_proj_router(x, w_out, h, g, w_router):
    t, d = h.shape
    k = x.shape[1]
    tm = TM_EPILOGUE
    w_pad = jnp.zeros((d, LANES), F32).at[:, :N_EXPERTS].set(w_router)
    w_hi = w_pad.astype(BF16)
    w_lo = (w_pad - w_hi.astype(F32)).astype(BF16)
    row = lambda w: pl.BlockSpec((tm, w), lambda i: (i, 0))
    full = lambda r, c: pl.BlockSpec((r, c), lambda i: (0, 0))
    return pl.pallas_call(
        _out_proj_router_body,
        grid=(t // tm,),
        in_specs=[row(k), full(k, d), row(d), full(1, d), full(d, LANES), full(d, LANES)],
        out_specs=[row(d), pl.BlockSpec((tm * ROW_TILE, LANES), lambda i: (i, 0)), row(LANES), row(LANES)],
        out_shape=[jax.ShapeDtypeStruct((t, d), F32), jax.ShapeDtypeStruct((t * ROW_TILE, LANES), F32),
                   jax.ShapeDtypeStruct((t, LANES), F32), jax.ShapeDtypeStruct((t, LANES), jnp.int32)],
        compiler_params=_params(1),
        name="sb_out_proj_router",
    )(x, w_out, h, g.reshape(1, d), w_hi, w_lo)


def _route_layout(idx, n_slots):
    n_tiles = n_slots // TM_GROUP
    flat = idx[:, :TOP_K].reshape(-1)
    onehot = (flat[:, None] == jnp.arange(N_EXPERTS, dtype=jnp.int32)[None, :]).astype(jnp.int32)
    csum = jnp.cumsum(onehot, axis=0)
    rank = jnp.sum(csum * onehot, axis=1) - 1
    counts = csum[-1]
    padded = ((counts + TM_GROUP - 1) // TM_GROUP) * TM_GROUP
    pad_end = jnp.cumsum(padded)
    group_start = pad_end - padded
    dest = group_start[flat] + rank
    pad_count = padded - counts
    pad_cum = jnp.cumsum(pad_count)
    k = jnp.arange(n_slots - flat.shape[0], dtype=jnp.int32)
    e_k = jnp.minimum(jnp.sum((k[:, None] >= pad_cum[None, :]).astype(jnp.int32), axis=1), N_EXPERTS - 1)
    in_group = (group_start + counts)[e_k] + k - (pad_cum - pad_count)[e_k]
    pad_slots = jnp.where(k < pad_cum[-1], in_group, pad_end[-1] + k - pad_cum[-1])
    tile_start = jnp.arange(n_tiles, dtype=jnp.int32) * TM_GROUP
    tile_expert = jnp.minimum(jnp.sum((tile_start[:, None] >= pad_end[None, :]).astype(jnp.int32), axis=1),
                              N_EXPERTS - 1)
    n_used = (pad_end[-1:] // TM_GROUP).astype(jnp.int32)
    return dest.reshape(-1, TOP_K), pad_slots, tile_expert, n_used


def _to_row_tiles(rows, out_ref):
    n, d = rows.shape
    for c in range(d // LANES):
        out_ref[pl.ds(c, n, stride=d // LANES), :] = rows[:, c * LANES:(c + 1) * LANES]


def _from_row_tiles(tiles_ref, n):
    return jnp.concatenate([tiles_ref[pl.ds(c, n, stride=ROW_TILE), :] for c in range(ROW_TILE)], axis=1)


def _row_copies(wait, idx_ref, n_rows, src_hbm, dst_for_row, sem):
    def copy(r):
        src = src_hbm.at[pl.ds(pl.multiple_of(idx_ref[0, 0, r] * ROW_TILE, ROW_TILE), ROW_TILE)]
        return pltpu.make_async_copy(src, dst_for_row(r), sem)

    def trip(g, c):
        for u in range(ROW_DMA_UNROLL):
            if wait:
                copy(g * ROW_DMA_UNROLL + u).wait()
            else:
                copy(g * ROW_DMA_UNROLL + u).start(priority=u % 2)
        return c

    lax.fori_loop(0, n_rows // ROW_DMA_UNROLL, trip, 0)


def _prefetched_row_gather(idx_ref, next_idx_ref, n_rows, src_hbm, dst_for_row, sems):
    i = pl.program_id(0)
    cur = i & 1

    @pl.when(i == 0)
    def _():
        _row_copies(False, idx_ref, n_rows, src_hbm, functools.partial(dst_for_row, 0), sems.at[0])

    @pl.when(i + 1 < pl.num_programs(0))
    def _():
        _row_copies(False, next_idx_ref, n_rows, src_hbm, functools.partial(dst_for_row, 1 - cur), sems.at[1 - cur])

    _row_copies(True, idx_ref, n_rows, src_hbm, functools.partial(dst_for_row, cur), sems.at[cur])
    return cur


def _tile_rows(r):
    return pl.ds(pl.multiple_of(r * ROW_TILE, ROW_TILE), ROW_TILE)


def _dispatch_body(dest_ref, x_ref, xs_hbm, sem):
    def copy(e):
        src = x_ref.at[_tile_rows(lax.shift_right_logical(e, 1))]
        return pltpu.make_async_copy(src, xs_hbm.at[_tile_rows(dest_ref[0, 0, e])], sem)

    def trip(wait, g, c):
        for u in range(ROW_DMA_UNROLL):
            if wait:
                copy(g * ROW_DMA_UNROLL + u).wait()
            else:
                copy(g * ROW_DMA_UNROLL + u).start(priority=u % 2)
        return c

    n_trips = dest_ref.shape[2] // ROW_DMA_UNROLL
    lax.fori_loop(0, n_trips, functools.partial(trip, False), 0)
    lax.fori_loop(0, n_trips, functools.partial(trip, True), 0)


def _dispatch_rows(x, dest, pad_dest):
    assert TOP_K == 2
    tm = TM_COMBINE
    n_tokens = x.shape[0] // ROW_TILE
    token_steps = n_tokens // tm
    all_dest = jnp.concatenate([dest.reshape(-1), pad_dest]).reshape(-1, 1, TOP_K * tm)
    n_slots = all_dest.size
    return pl.pallas_call(
        _dispatch_body,
        grid=(all_dest.shape[0],),
        in_specs=[pl.BlockSpec((1, 1, TOP_K * tm), lambda i: (i, 0, 0), memory_space=pltpu.SMEM),
                  pl.BlockSpec((tm * ROW_TILE, LANES), lambda i: (jnp.minimum(i, token_steps - 1), 0))],
        out_specs=pl.BlockSpec(memory_space=pl.ANY),
        out_shape=jax.ShapeDtypeStruct((n_slots * ROW_TILE, LANES), F32),
        scratch_shapes=[pltpu.SemaphoreType.DMA(())],
        compiler_params=_params(1),
        name="moe_dispatch",
    )(all_dest, x)


def _stage_expert_weight(te_ref, w_hbm, layer, w_bf, stage, sems, chunk_of):
    t = pl.program_id(0)
    expert = te_ref[t]
    n_chunks = w_bf.size // stage[0].size
    n_bufs = stage.shape[0]

    @pl.when(jnp.logical_or(t == 0, expert != te_ref[jnp.maximum(t - 1, 0)]))
    def _():
        def copy(c):
            buf = c % n_bufs
            return pltpu.make_async_copy(chunk_of(w_hbm.at[layer, expert], c), stage.at[buf], sems.at[buf])

        for c in range(min(n_bufs - 1, n_chunks)):
            copy(c).start()
        for c in range(n_chunks):
            copy(c).wait()
            chunk_of(w_bf, c)[...] = stage[c % n_bufs].astype(w_bf.dtype)
            if c + n_bufs - 1 < n_chunks:
                copy(c + n_bufs - 1).start()


def _group_swiglu_body(te_ref, nu_ref, x_ref, w_hbm, o_ref, w_bf, stage, sems, *, layer):
    used = pl.program_id(0) < nu_ref[0]

    @pl.when(used)
    def _():
        _stage_expert_weight(te_ref, w_hbm, layer, w_bf, stage, sems,
                             lambda ref, c: ref.at[:, pl.ds(c * TN_FF, TN_FF)])
        _swiglu_cols(_from_row_tiles(x_ref, TM_GROUP).astype(BF16), w_bf, o_ref)

    @pl.when(jnp.logical_not(used))
    def _():
        o_ref[...] = jnp.zeros_like(o_ref)


def _group_down_body(te_ref, nu_ref, x_ref, w_hbm, o_ref, w_bf, stage, sems, *, layer):
    used = pl.program_id(0) < nu_ref[0]

    @pl.when(used)
    def _():
        rows = stage.shape[1]
        _stage_expert_weight(te_ref, w_hbm, layer, w_bf, stage, sems,
                             lambda ref, c: ref.at[pl.ds(c * rows, rows), :])
        _to_row_tiles(_dot(x_ref[...], w_bf[...]), o_ref)

    @pl.when(jnp.logical_not(used))
    def _():
        o_ref[...] = jnp.zeros_like(o_ref)


def _group_swiglu_up(xs, w_gate_up, layer, tile_expert, n_used):
    n_slots = xs.shape[0] // ROW_TILE
    k = ROW_TILE * LANES
    n_tiles = n_slots // TM_GROUP
    last = lambda t, nu: jnp.minimum(t, nu[0] - 1)
    grid_spec = pltpu.PrefetchScalarGridSpec(
        num_scalar_prefetch=2,
        grid=(n_tiles,),
        in_specs=[pl.BlockSpec((TM_GROUP * ROW_TILE, LANES), lambda t, te, nu: (last(t, nu), 0)),
                  pl.BlockSpec(memory_space=pl.ANY)],
        out_specs=pl.BlockSpec((TM_GROUP, D_FF), lambda t, te, nu: (t, 0)),
        scratch_shapes=[pltpu.VMEM((k, 2 * D_FF), BF16), pltpu.VMEM((WEIGHT_STAGE_BUFFERS, k, TN_FF), F32),
                        pltpu.SemaphoreType.DMA((WEIGHT_STAGE_BUFFERS,))],
    )
    return pl.pallas_call(
        functools.partial(_group_swiglu_body, layer=layer),
        grid_spec=grid_spec,
        out_shape=jax.ShapeDtypeStruct((n_slots, D_FF), BF16),
        compiler_params=_params(1),
        name="moe_swiglu_up",
    )(tile_expert, n_used, xs, w_gate_up)


def _group_down(act, w_down, layer, tile_expert, n_used):
    n_slots, k = act.shape
    n = w_down.shape[-1]
    n_tiles = n_slots // TM_GROUP
    last = lambda t, nu: jnp.minimum(t, nu[0] - 1)
    grid_spec = pltpu.PrefetchScalarGridSpec(
        num_scalar_prefetch=2,
        grid=(n_tiles,),
        in_specs=[pl.BlockSpec((TM_GROUP, k), lambda t, te, nu: (last(t, nu), 0)),
                  pl.BlockSpec(memory_space=pl.ANY)],
        out_specs=pl.BlockSpec((TM_GROUP * ROW_TILE, LANES), lambda t, te, nu: (t, 0)),
        scratch_shapes=[pltpu.VMEM((k, n), BF16), pltpu.VMEM((WEIGHT_STAGE_BUFFERS, TN_FF, n), F32),
                        pltpu.SemaphoreType.DMA((WEIGHT_STAGE_BUFFERS,))],
    )
    return pl.pallas_call(
        functools.partial(_group_down_body, layer=layer),
        grid_spec=grid_spec,
        out_shape=jax.ShapeDtypeStruct((n_slots * ROW_TILE, LANES), F32),
        compiler_params=_params(1),
        name="moe_down",
    )(tile_expert, n_used, act, w_down)


def _combined_rows(slot_ref, next_slot_ref, gate_ref, h_ref, y_hbm, buf, sems):
    cur = _prefetched_row_gather(
        slot_ref, next_slot_ref, TOP_K * TM_COMBINE, y_hbm,
        lambda b, e: buf.at[b, e & 1, _tile_rows(lax.shift_right_logical(e, 1))], sems)
    gates = gate_ref[...]
    best, second = [_from_row_tiles(buf.at[cur, c], TM_COMBINE) for c in range(TOP_K)]
    return h_ref[...] + gates[:, 0:1] * best + gates[:, 1:2] * second


def _moe_combine_body(slot_ref, next_slot_ref, gate_ref, h_ref, y_hbm, out_ref, buf, sems):
    out_ref[...] = _combined_rows(slot_ref, next_slot_ref, gate_ref, h_ref, y_hbm, buf, sems)


def _moe_combine_norm_body(slot_ref, next_slot_ref, gate_ref, h_ref, g_ref, y_hbm, out_ref, buf, sems):
    rows = _combined_rows(slot_ref, next_slot_ref, gate_ref, h_ref, y_hbm, buf, sems)
    out_ref[...] = _rmsnorm_rows(rows, g_ref[...])


def _moe_combine(h, y_slots, slots, gates, g_final=None):
    t, d = h.shape
    tm = TM_COMBINE
    n_tiles = t // tm
    slot_idx = slots.reshape(n_tiles, 1, TOP_K * tm)
    in_specs = [pl.BlockSpec((1, 1, TOP_K * tm), lambda i: (i, 0, 0), memory_space=pltpu.SMEM),
                pl.BlockSpec((1, 1, TOP_K * tm), lambda i: (jnp.minimum(i + 1, n_tiles - 1), 0, 0),
                             memory_space=pltpu.SMEM),
                pl.BlockSpec((tm, LANES), lambda i: (i, 0)),
                pl.BlockSpec((tm, d), lambda i: (i, 0))]
    args = [slot_idx, slot_idx, gates, h]
    if g_final is not None:
        in_specs.append(pl.BlockSpec((1, d), lambda i: (0, 0)))
        args.append(g_final.reshape(1, d))
    return pl.pallas_call(
        _moe_combine_body if g_final is None else _moe_combine_norm_body,
        grid=(n_tiles,),
        in_specs=in_specs + [pl.BlockSpec(memory_space=pl.ANY)],
        out_specs=pl.BlockSpec((tm, d), lambda i: (i, 0)),
        out_shape=jax.ShapeDtypeStruct((t, d), F32),
        scratch_shapes=[pltpu.VMEM((2, TOP_K, tm * ROW_TILE, LANES), F32), pltpu.SemaphoreType.DMA((2,))],
        compiler_params=_params(1),
        name="moe_combine",
    )(*args, y_slots)


def _moe_layer(h, hn, gates, idx, w_gate_up, w_down, layer, g_final=None):
    t = h.shape[0]
    n_slots = t * TOP_K + N_EXPERTS * TM_GROUP
    slots, pad_slots, tile_expert, n_used = _route_layout(idx, n_slots)
    xs = _dispatch_rows(hn, slots, pad_slots)
    act = _group_swiglu_up(xs, w_gate_up, layer, tile_expert, n_used)
    y_slots = _group_down(act, w_down, layer, tile_expert, n_used)
    return _moe_combine(h, y_slots, slots, gates, g_final)


def kernel(x, mix_norm, ffn_norm, dil_w_in, dil_w_out, sb_w_qkv, sb_w_out, ffn_w_gate_up, ffn_w_down,
           moe_w_router, moe_w_gate_up, moe_w_down, final_norm):
    batch, seq, d = x.shape
    depth = mix_norm.shape[0]
    h = x.reshape(batch * seq, d)
    hn = None
    for i in range(depth):
        j = i // 2
        last = i + 1 == depth
        if i % 2 == 0:
            h, hn = _dilated_mixer(h, mix_norm[i], dil_w_in[j], dil_w_out[j], ffn_norm[i], batch, seq)
            act = _swiglu_up(hn, ffn_w_gate_up[j].astype(BF16))
            if last:
                h, hn = _matmul_residual(act, ffn_w_down[j].astype(BF16), h, "ffn_down"), None
            else:
                h, hn = _matmul_residual(act, ffn_w_down[j].astype(BF16), h, "ffn_down", g_next=mix_norm[i + 1])
        else:
            if hn is None:
                hn = _rmsnorm(h, mix_norm[i], BF16)
            qkv = _matmul(hn, sb_w_qkv[j].astype(BF16), BF16, "sb_qkv_proj")
            y = _stick_breaking_attention(qkv, batch, seq)
            h, hn_tiles, gates, idx = _out_proj_router(y, sb_w_out[j].astype(BF16), h, ffn_norm[i], moe_w_router[j])
            h = _moe_layer(h, hn_tiles, gates, idx, moe_w_gate_up, moe_w_down, j,
                           g_final=final_norm if last else None)
            hn = None
            if last:
                return h.reshape(batch, seq, d)
    return _rmsnorm(h, final_norm, F32).reshape(batch, seq, d)
```

```python
import functools

import numpy as np
import jax
import jax.numpy as jnp
from jax import lax
from jax.experimental import pallas as pl
from jax.experimental.pallas import tpu as pltpu

D_MODEL = 1024
N_HEADS = 16
HEAD_DIM = 64
D_ATTN = N_HEADS * HEAD_DIM
DILATED_GROUPS = ((128, 1), (512, 4), (2048, 16))
N_GROUPS = len(DILATED_GROUPS)
BLOCK = 128
D_FF = 3584
N_EXPERTS = 8
TOP_K = 2
RMS_EPS = 1e-6
ATTN_SCALE = 0.125
NEG_LOG2E = -float(np.float32(np.log2(np.e)))

LANES = 128
VMEM_LIMIT_BYTES = 56 * 1024 * 1024

TM = 1024
TN = 1024
TN_FF = 512
TM_GROUP = 512
TM_COMBINE = 256
TM_EPILOGUE = 256
SB_CHUNK = 256
SB_HALVES = 2
SB_DEAD_CARRY = -104.0
DIL_PAIRS_PER_STAGE = 4
ROW_DMA_UNROLL = 8
ROW_TILE = D_MODEL // LANES
WEIGHT_STAGE_BUFFERS = 4

F32 = jnp.float32
BF16 = jnp.bfloat16

ALIBI_SLOPES = tuple(float(np.float32(2.0 ** (-8.0 * (h + 1) / N_HEADS))) for h in range(N_HEADS))


def _params(n_axes):
    return pltpu.CompilerParams(dimension_semantics=("arbitrary",) * n_axes, vmem_limit_bytes=VMEM_LIMIT_BYTES)


def _dot(a, b):
    return jnp.dot(a, b, preferred_element_type=F32)


def _dot_nt(a, b):
    return lax.dot_general(a, b, (((1,), (1,)), ((), ())), preferred_element_type=F32)


def _split_bf16(x):
    hi = x.astype(BF16)
    lo = (x - hi.astype(F32)).astype(BF16)
    return hi, lo


def _rmsnorm_rows(x, g):
    return x * lax.rsqrt(jnp.mean(x * x, axis=-1, keepdims=True) + RMS_EPS) * g


def _rmsnorm_body(x_ref, g_ref, o_ref):
    o_ref[...] = _rmsnorm_rows(x_ref[...], g_ref[...]).astype(o_ref.dtype)


def _rmsnorm(x, g, out_dtype):
    t, d = x.shape
    return pl.pallas_call(
        _rmsnorm_body,
        grid=(t // TM,),
        in_specs=[pl.BlockSpec((TM, d), lambda i: (i, 0)), pl.BlockSpec((1, d), lambda i: (0, 0))],
        out_specs=pl.BlockSpec((TM, d), lambda i: (i, 0)),
        out_shape=jax.ShapeDtypeStruct((t, d), out_dtype),
        compiler_params=_params(1),
        name="rmsnorm",
    )(x, g.reshape(1, d))


def _rmsnorm_residue_body(x_ref, g_ref, *refs):
    o_refs, lanes_ref = refs[:-1], refs[-1]
    y = _rmsnorm_rows(x_ref[0], g_ref[...])
    n_chunks = y.shape[1] // LANES
    for c in range(n_chunks):
        lanes_ref[c] = y[:, c * LANES:(c + 1) * LANES]
    for o_ref, (_, dilation) in zip(o_refs, DILATED_GROUPS):
        if dilation == 1:
            o_ref[0, 0] = y.astype(o_ref.dtype)
            continue
        rows = TM // dilation
        for r in range(dilation):
            for c in range(n_chunks):
                picked = lanes_ref[c, pl.ds(r, rows, stride=dilation), :]
                o_ref[0, r, :, c * LANES:(c + 1) * LANES] = picked.astype(o_ref.dtype)


def _rmsnorm_by_residue(x, g, batch, seq):
    d_model = x.shape[1]
    tiles = seq // TM
    out_specs, out_shape = [], []
    for _, dilation in DILATED_GROUPS:
        out_specs.append(pl.BlockSpec((1, dilation, TM // dilation, d_model), lambda b, i: (b, 0, i, 0)))
        out_shape.append(jax.ShapeDtypeStruct((batch, dilation, seq // dilation, d_model), BF16))
    outs = pl.pallas_call(
        _rmsnorm_residue_body,
        grid=(batch, tiles),
        in_specs=[pl.BlockSpec((1, TM, d_model), lambda b, i: (b, i, 0)),
                  pl.BlockSpec((1, d_model), lambda b, i: (0, 0))],
        out_specs=out_specs,
        out_shape=out_shape,
        scratch_shapes=[pltpu.VMEM((d_model // LANES, TM, LANES), F32)],
        compiler_params=_params(2),
        name="rmsnorm_by_residue",
    )(x.reshape(batch, seq, d_model), g.reshape(1, d_model))
    return [o.reshape(batch * seq, d_model) for o in outs]


def _mm_body(x_ref, w_ref, o_ref):
    x = x_ref[...]
    for c in range(0, o_ref.shape[1], TN):
        o_ref[:, c:c + TN] = _dot(x, w_ref[:, c:c + TN]).astype(o_ref.dtype)


def _mm_res_body(x_ref, w_ref, r_ref, o_ref):
    o_ref[...] = r_ref[...] + _dot(x_ref[...], w_ref[...])


def _mm_res_norm_body(x_ref, w_ref, r_ref, g_ref, o_ref, hn_ref):
    h_new = r_ref[...] + _dot(x_ref[...], w_ref[...])
    o_ref[...] = h_new
    hn_ref[...] = _rmsnorm_rows(h_new, g_ref[...]).astype(hn_ref.dtype)


def _matmul(x, w, out_dtype, name, first_col=0, n_cols=None):
    m, k = x.shape
    n = w.shape[1] if n_cols is None else n_cols
    panel = first_col // n
    return pl.pallas_call(
        _mm_body,
        grid=(m // TM,),
        in_specs=[pl.BlockSpec((TM, k), lambda i: (i, 0)), pl.BlockSpec((k, n), lambda i: (0, panel))],
        out_specs=pl.BlockSpec((TM, n), lambda i: (i, 0)),
        out_shape=jax.ShapeDtypeStruct((m, n), out_dtype),
        compiler_params=_params(1),
        name=name,
    )(x, w)


def _matmul_residual(x, w, res, name, g_next=None):
    m, k = x.shape
    n = w.shape[1]
    row = pl.BlockSpec((TM, n), lambda i: (i, 0))
    in_specs = [pl.BlockSpec((TM, k), lambda i: (i, 0)), pl.BlockSpec((k, n), lambda i: (0, 0)), row]
    if g_next is None:
        return pl.pallas_call(
            _mm_res_body,
            grid=(m // TM,),
            in_specs=in_specs,
            out_specs=row,
            out_shape=jax.ShapeDtypeStruct((m, n), F32),
            compiler_params=_params(1),
            name=name,
        )(x, w, res)
    return pl.pallas_call(
        _mm_res_norm_body,
        grid=(m // TM,),
        in_specs=in_specs + [pl.BlockSpec((1, n), lambda i: (0, 0))],
        out_specs=[row, row],
        out_shape=[jax.ShapeDtypeStruct((m, n), F32), jax.ShapeDtypeStruct((m, n), BF16)],
        compiler_params=_params(1),
        name=name,
    )(x, w, res, g_next.reshape(1, n))


def _swiglu(g, u):
    return g * (1.0 / (1.0 + jnp.exp(-g))) * u


def _swiglu_cols(x, w_gate_up, o_ref):
    for c in range(0, D_FF, TN_FF):
        gate = _dot(x, w_gate_up[:, c:c + TN_FF])
        up = _dot(x, w_gate_up[:, D_FF + c:D_FF + c + TN_FF])
        o_ref[:, c:c + TN_FF] = _swiglu(gate, up).astype(o_ref.dtype)


def _swiglu_body(x_ref, w_ref, o_ref):
    _swiglu_cols(x_ref[...], w_ref, o_ref)


def _swiglu_up(x, w_gate_up):
    m, k = x.shape
    return pl.pallas_call(
        _swiglu_body,
        grid=(m // TM,),
        in_specs=[pl.BlockSpec((TM, k), lambda i: (i, 0)),
                  pl.BlockSpec((k, 2 * D_FF), lambda i: (0, 0))],
        out_specs=pl.BlockSpec((TM, D_FF), lambda i: (i, 0)),
        out_shape=jax.ShapeDtypeStruct((m, D_FF), BF16),
        compiler_params=_params(1),
        name="swiglu_up",
    )(x, w_gate_up)


def _alibi_window_bias(dilation):
    a = np.arange(BLOCK)[:, None]
    m = np.arange(2 * BLOCK)[None, :]
    rel = BLOCK + a - m
    in_window = (rel >= 0) & (rel <= BLOCK)
    slopes = np.asarray(ALIBI_SLOPES, np.float32)[:, None, None]
    bias = -slopes * (rel * dilation).astype(np.float32)[None]
    later = np.where(in_window[None], bias, -np.inf)
    first = np.where((m >= BLOCK)[None], later, -np.inf)
    return np.stack([first, later]).astype(np.float32)


def _dil_attn_body(q_ref, kc_ref, kp_ref, vc_ref, vp_ref, bias_ref, o_ref, lse_ref):
    n_keys = 2 * BLOCK
    lane = lax.broadcasted_iota(jnp.int32, (BLOCK, LANES), 1)
    first_head = lane < HEAD_DIM
    first_head_keys = lax.broadcasted_iota(jnp.int32, (n_keys, LANES), 1) < HEAD_DIM
    lse_tile = jnp.zeros((BLOCK, LANES), F32)

    def by_head(pair_rows):
        zero = jnp.zeros_like(pair_rows)
        return jnp.concatenate([jnp.where(first_head_keys, pair_rows, zero),
                                jnp.where(first_head_keys, zero, pair_rows)], axis=0)

    for first_pair in range(0, N_HEADS // 2, DIL_PAIRS_PER_STAGE):
        pairs = range(first_pair, first_pair + DIL_PAIRS_PER_STAGE)
        cols = {p: slice(p * LANES, (p + 1) * LANES) for p in pairs}
        scores = {p: _dot_nt(q_ref[:, cols[p]] * ATTN_SCALE,
                             by_head(jnp.concatenate([kp_ref[:, cols[p]], kc_ref[:, cols[p]]], axis=0)))
                  for p in pairs}
        probs, norms = {}, {}
        for p in pairs:
            pair_probs = []
            for hh in range(2):
                h = 2 * p + hh
                s = scores[p][:, hh * n_keys:(hh + 1) * n_keys] + bias_ref[0, h]
                s_max = jnp.max(s, axis=1, keepdims=True)
                e = jnp.exp(s - s_max)
                z = jnp.sum(e, axis=1, keepdims=True)
                lse_tile = jnp.where(lane == h, s_max + jnp.log(z), lse_tile)
                pair_probs.append(e.astype(BF16))
                norms[h] = z
            probs[p] = jnp.concatenate(pair_probs, axis=1)
        for p in pairs:
            v_heads = by_head(jnp.concatenate([vp_ref[:, cols[p]], vc_ref[:, cols[p]]], axis=0))
            o_pair = _dot(probs[p], v_heads) / jnp.where(first_head, norms[2 * p], norms[2 * p + 1])
            o_ref[:, cols[p]] = o_pair.astype(o_ref.dtype)
    lse_ref[...] = lse_tile


def _dilated_attention(proj, batch, seq, group):
    _, dilation = DILATED_GROUPS[group]
    n_blocks = seq // dilation // BLOCK

    def block(which, back):
        return lambda b, r, n: ((b * dilation + r) * n_blocks + jnp.maximum(n - back, 0), which)

    blk = (BLOCK, D_ATTN)
    bias = jnp.asarray(_alibi_window_bias(dilation))
    return pl.pallas_call(
        _dil_attn_body,
        grid=(batch, dilation, n_blocks),
        in_specs=[pl.BlockSpec(blk, block(0, 0)), pl.BlockSpec(blk, block(1, 0)), pl.BlockSpec(blk, block(1, 1)),
                  pl.BlockSpec(blk, block(2, 0)), pl.BlockSpec(blk, block(2, 1)),
                  pl.BlockSpec((1,) + bias.shape[1:], lambda b, r, n: (jnp.minimum(n, 1), 0, 0, 0))],
        out_specs=[pl.BlockSpec(blk, block(0, 0)), pl.BlockSpec((BLOCK, LANES), block(0, 0))],
        out_shape=[jax.ShapeDtypeStruct((batch * seq, D_ATTN), BF16),
                   jax.ShapeDtypeStruct((batch * seq, LANES), F32)],
        compiler_params=_params(3),
        name=f"dilated_attn_g{group}",
    )(proj, proj, proj, proj, proj, bias)


def _dil_out_body(o0_ref, o1_ref, o2_ref, l0_ref, l1_ref, l2_ref, e_ref, w_ref, h_ref, g_ref, out_ref, hn_ref,
                  rows_ref, stat_ref):
    tile = h_ref.shape[1]
    n_chunks = D_ATTN // LANES
    outs, lses = [], []
    for grp, (o_ref, l_ref) in enumerate(zip((o0_ref, o1_ref, o2_ref), (l0_ref, l1_ref, l2_ref))):
        dilation = DILATED_GROUPS[grp][1]
        if dilation == 1:
            outs.append(o_ref[0, 0].astype(F32))
            lses.append(l_ref[0, 0])
            continue
        for r in range(dilation):
            rows = pl.ds(r, tile // dilation, stride=dilation)
            stat_ref[grp, rows, :] = l_ref[0, r]
            for c in range(n_chunks):
                rows_ref[grp, c, rows, :] = o_ref[0, r, :, c * LANES:(c + 1) * LANES].astype(F32)
        outs.append(jnp.concatenate([rows_ref[grp, c] for c in range(n_chunks)], axis=1))
        lses.append(stat_ref[grp])
    top = jnp.maximum(jnp.maximum(lses[0], lses[1]), lses[2])
    es = [jnp.exp(l - top) for l in lses]
    den = es[0] + es[1] + es[2]
    expand = e_ref[...]
    y = None
    for grp in range(N_GROUPS):
        term = _dot((es[grp] / den).astype(BF16), expand) * outs[grp]
        y = term if y is None else y + term
    h_new = h_ref[0] + _dot(y.astype(BF16), w_ref[...])
    out_ref[0] = h_new
    hn_ref[0] = _rmsnorm_rows(h_new, g_ref[...]).astype(hn_ref.dtype)


def _dilated_out_proj(outs, lses, w_out, h, g_next, batch, seq):
    tm = TM_EPILOGUE
    expand = (np.arange(LANES)[:, None] == (np.arange(D_ATTN)[None, :] // HEAD_DIM)).astype(np.float32)

    def by_residue(width, dilation):
        return pl.BlockSpec((1, dilation, tm // dilation, width), lambda b, i: (b, 0, i, 0))

    def grouped(arrays, width):
        return [a.reshape(batch, d, seq // d, width) for a, (_, d) in zip(arrays, DILATED_GROUPS)]

    row = pl.BlockSpec((1, tm, D_MODEL), lambda b, i: (b, i, 0))
    full = lambda r, c: pl.BlockSpec((r, c), lambda b, i: (0, 0))
    h_new, hn = pl.pallas_call(
        _dil_out_body,
        grid=(batch, seq // tm),
        in_specs=[by_residue(D_ATTN, d) for _, d in DILATED_GROUPS]
        + [by_residue(LANES, d) for _, d in DILATED_GROUPS]
        + [full(LANES, D_ATTN), full(D_ATTN, D_MODEL), row, full(1, D_MODEL)],
        out_specs=[row, row],
        out_shape=[jax.ShapeDtypeStruct((batch, seq, D_MODEL), F32),
                   jax.ShapeDtypeStruct((batch, seq, D_MODEL), BF16)],
        scratch_shapes=[pltpu.VMEM((N_GROUPS, D_ATTN // LANES, tm, LANES), F32),
                        pltpu.VMEM((N_GROUPS, tm, LANES), F32)],
        compiler_params=_params(2),
        name="dilated_out_proj",
    )(*grouped(outs, D_ATTN), *grouped(lses, LANES), jnp.asarray(expand, BF16), w_out,
      h.reshape(batch, seq, D_MODEL), g_next.reshape(1, D_MODEL))
    return h_new.reshape(batch * seq, D_MODEL), hn.reshape(batch * seq, D_MODEL)


def _dilated_mixer(h, g, w_in, w_out, g_next, batch, seq):
    w_in = w_in.astype(BF16)
    parts = []
    for group, hn in enumerate(_rmsnorm_by_residue(h, g, batch, seq)):
        proj = _matmul(hn, w_in, BF16, f"dilated_in_proj_g{group}", first_col=group * 3 * D_ATTN, n_cols=3 * D_ATTN)
        parts.append(_dilated_attention(proj, batch, seq, group))
    return _dilated_out_proj([p[0] for p in parts], [p[1] for p in parts], w_out.astype(BF16), h, g_next,
                             batch, seq)


def _sb_attn_body(q_ref, k_ref, v_ref, tri_ref, o_ref):
    i = pl.program_id(2)
    row = lax.broadcasted_iota(jnp.int32, (SB_CHUNK, SB_CHUNK), 0)
    col = lax.broadcasted_iota(jnp.int32, (SB_CHUNK, SB_CHUNK), 1)
    strict = col < row
    first_head = lax.broadcasted_iota(jnp.int32, (SB_CHUNK, LANES), 1) < HEAD_DIM
    head_lanes = (first_head, ~first_head)
    tri = tri_ref[...]
    heads = range(2)
    q_heads = []
    for half in range(SB_HALVES):
        q_pair = q_ref[0, half * SB_CHUNK:(half + 1) * SB_CHUNK, :] * ATTN_SCALE
        q_heads.append([jnp.where(sel, q_pair, jnp.zeros_like(q_pair)) for sel in head_lanes])

    def advance(state, work):
        def rows(ref, j):
            return ref[0, pl.ds(pl.multiple_of(j * SB_CHUNK, SB_CHUNK), SB_CHUNK), :]

        chunks = list({id(item[0]): item[0] for item in work}.values())
        k_of = {id(j): rows(k_ref, j) for j in chunks}
        v_of = {id(j): [jnp.where(sel, v, jnp.zeros_like(v)) for sel in head_lanes]
                for j in chunks for v in [rows(v_ref, j)]}
        chains = [(j, half, hh, diag) for j, half, diag in work for hh in heads]
        zs = [_dot_nt(q_heads[half][hh], k_of[id(j)]) for j, half, hh, _ in chains]
        zs = [jnp.where(strict, z, -1e30) if diag else z for z, (_, _, _, diag) in zip(zs, chains)]
        sps = [jnp.maximum(z, 0.0) + jnp.log(1.0 + jnp.exp2(jnp.abs(z) * NEG_LOG2E)) for z in zs]
        later = [_dot(sp.astype(BF16), tri) for sp in sps]
        new_state = list(state)
        for c, (j, half, hh, _) in enumerate(chains):
            carries, acc = new_state[half]
            att = jnp.exp((zs[c] - sps[c]) + later[c] + carries[hh]).astype(BF16)
            chunk_sum = later[c][:, 0:1] - sps[c][:, 0:1]
            carries = tuple(carries[x] + chunk_sum if x == hh else carries[x] for x in heads)
            new_state[half] = (carries, acc + _dot(att, v_of[id(j)][hh]))
        return tuple(new_state)

    def max_carry(state, half):
        return functools.reduce(jnp.maximum, [jnp.max(c) for c in state[half][0]])

    zero_carry = jnp.zeros((SB_CHUNK, 1), F32)
    state = (((zero_carry, zero_carry), jnp.zeros((SB_CHUNK, LANES), F32)),) * SB_HALVES
    diag1, diag0 = SB_HALVES * i + 1, SB_HALVES * i
    state = advance(state, ((diag1, 1, True), (diag0, 0, True), (diag0, 1, False)))

    def walk(halves, watched, start):
        def live(loop):
            jj, _, carry_bound = loop
            return jnp.logical_and(jj < SB_HALVES * i, carry_bound > SB_DEAD_CARRY)

        def step(loop):
            jj, state, _ = loop
            j = SB_HALVES * i - 1 - jj
            state = advance(state, tuple((j, half, False) for half in halves))
            return jj + 1, state, max_carry(state, watched)

        jj, state = start
        jj, state, _ = lax.while_loop(live, step, (jj, state, max_carry(state, watched)))
        return jj, state

    _, state = walk((0,), 0, walk((0, 1), 1, (jnp.int32(0), state)))
    for half in range(SB_HALVES):
        o_ref[0, half * SB_CHUNK:(half + 1) * SB_CHUNK, :] = state[half][1].astype(o_ref.dtype)


def _stick_breaking_attention(qkv, batch, seq):
    n_pairs = N_HEADS // 2
    view = qkv.reshape(batch, seq, 3 * D_ATTN)
    q_rows = SB_HALVES * SB_CHUNK
    j = np.arange(SB_CHUNK)
    tri = -(j[:, None] > j[None, :]).astype(np.float32)
    out = pl.pallas_call(
        _sb_attn_body,
        grid=(batch, n_pairs, seq // q_rows),
        in_specs=[pl.BlockSpec((1, q_rows, LANES), lambda b, p, i: (b, i, p)),
                  pl.BlockSpec((1, seq, LANES), lambda b, p, i: (b, 0, n_pairs + p)),
                  pl.BlockSpec((1, seq, LANES), lambda b, p, i: (b, 0, 2 * n_pairs + p)),
                  pl.BlockSpec((SB_CHUNK, SB_CHUNK), lambda b, p, i: (0, 0))],
        out_specs=pl.BlockSpec((1, q_rows, LANES), lambda b, p, i: (b, i, p)),
        out_shape=jax.ShapeDtypeStruct((batch, seq, D_ATTN), BF16),
        compiler_params=_params(3),
        name="stick_breaking_attn",
    )(view, view, view, jnp.asarray(tri, BF16))
    return out.reshape(batch * seq, D_ATTN)


def _out_proj_router_body(x_ref, w_ref, r_ref, g_ref, wh_ref, wl_ref, out_ref, hn_ref, gate_ref, idx_ref):
    h_new = r_ref[...] + _dot(x_ref[...], w_ref[...])
    out_ref[...] = h_new
    y = _rmsnorm_rows(h_new, g_ref[...])
    _to_row_tiles(y, hn_ref)
    y_hi, y_lo = _split_bf16(y)
    w_hi, w_lo = wh_ref[...], wl_ref[...]
    logits = _dot(y_hi, w_hi) + _dot(y_hi, w_lo) + _dot(y_lo, w_hi)
    lane = lax.broadcasted_iota(jnp.int32, logits.shape, 1)
    neg_inf = F32(-jnp.inf)
    logits = jnp.where(lane < N_EXPERTS, logits, neg_inf)
    v1 = jnp.max(logits, axis=1, keepdims=True)
    i1 = jnp.min(jnp.where(logits == v1, lane, LANES), axis=1, keepdims=True)
    rest = jnp.where(lane == i1, neg_inf, logits)
    v2 = jnp.max(rest, axis=1, keepdims=True)
    i2 = jnp.min(jnp.where(rest == v2, lane, LANES), axis=1, keepdims=True)
    e2 = jnp.exp(v2 - v1)
    w1 = 1.0 / (1.0 + e2)
    w2 = e2 * w1
    gate_ref[...] = jnp.where(lane == 0, w1, jnp.where(lane == 1, w2, 0.0))
    idx_ref[...] = jnp.where(lane == 0, i1, jnp.where(lane == 1, i2, 0))


def _out_proj_router(x, w_out, h, g, w_router):
    t, d = h.shape
    k = x.shape[1]
    tm = TM_EPILOGUE
    w_pad = jnp.zeros((d, LANES), F32).at[:, :N_EXPERTS].set(w_router)
    w_hi = w_pad.astype(BF16)
    w_lo = (w_pad - w_hi.astype(F32)).astype(BF16)
    row = lambda w: pl.BlockSpec((tm, w), lambda i: (i, 0))
    full = lambda r, c: pl.BlockSpec((r, c), lambda i: (0, 0))
    return pl.pallas_call(
        _out_proj_router_body,
        grid=(t // tm,),
        in_specs=[row(k), full(k, d), row(d), full(1, d), full(d, LANES), full(d, LANES)],
        out_specs=[row(d), pl.BlockSpec((tm * ROW_TILE, LANES), lambda i: (i, 0)), row(LANES), row(LANES)],
        out_shape=[jax.ShapeDtypeStruct((t, d), F32), jax.ShapeDtypeStruct((t * ROW_TILE, LANES), F32),
                   jax.ShapeDtypeStruct((t, LANES), F32), jax.ShapeDtypeStruct((t, LANES), jnp.int32)],
        compiler_params=_params(1),
        name="sb_out_proj_router",
    )(x, w_out, h, g.reshape(1, d), w_hi, w_lo)


def _route_layout(idx, n_slots):
    n_tiles = n_slots // TM_GROUP
    flat = idx[:, :TOP_K].reshape(-1)
    onehot = (flat[:, None] == jnp.arange(N_EXPERTS, dtype=jnp.int32)[None, :]).astype(jnp.int32)
    csum = jnp.cumsum(onehot, axis=0)
    rank = jnp.sum(csum * onehot, axis=1) - 1
    counts = csum[-1]
    padded = ((counts + TM_GROUP - 1) // TM_GROUP) * TM_GROUP
    pad_end = jnp.cumsum(padded)
    group_start = pad_end - padded
    dest = group_start[flat] + rank
    pad_count = padded - counts
    pad_cum = jnp.cumsum(pad_count)
    k = jnp.arange(n_slots - flat.shape[0], dtype=jnp.int32)
    e_k = jnp.minimum(jnp.sum((k[:, None] >= pad_cum[None, :]).astype(jnp.int32), axis=1), N_EXPERTS - 1)
    in_group = (group_start + counts)[e_k] + k - (pad_cum - pad_count)[e_k]
    pad_slots = jnp.where(k < pad_cum[-1], in_group, pad_end[-1] + k - pad_cum[-1])
    tile_start = jnp.arange(n_tiles, dtype=jnp.int32) * TM_GROUP
    tile_expert = jnp.minimum(jnp.sum((tile_start[:, None] >= pad_end[None, :]).astype(jnp.int32), axis=1),
                              N_EXPERTS - 1)
    n_used = (pad_end[-1:] // TM_GROUP).astype(jnp.int32)
    return dest.reshape(-1, TOP_K), pad_slots, tile_expert, n_used


def _to_row_tiles(rows, out_ref):
    n, d = rows.shape
    for c in range(d // LANES):
        out_ref[pl.ds(c, n, stride=d // LANES), :] = rows[:, c * LANES:(c + 1) * LANES]


def _from_row_tiles(tiles_ref, n):
    return jnp.concatenate([tiles_ref[pl.ds(c, n, stride=ROW_TILE), :] for c in range(ROW_TILE)], axis=1)


def _row_copies(wait, idx_ref, n_rows, src_hbm, dst_for_row, sem):
    def copy(r):
        src = src_hbm.at[pl.ds(pl.multiple_of(idx_ref[0, 0, r] * ROW_TILE, ROW_TILE), ROW_TILE)]
        return pltpu.make_async_copy(src, dst_for_row(r), sem)

    def trip(g, c):
        for u in range(ROW_DMA_UNROLL):
            if wait:
                copy(g * ROW_DMA_UNROLL + u).wait()
            else:
                copy(g * ROW_DMA_UNROLL + u).start(priority=u % 2)
        return c

    lax.fori_loop(0, n_rows // ROW_DMA_UNROLL, trip, 0)


def _prefetched_row_gather(idx_ref, next_idx_ref, n_rows, src_hbm, dst_for_row, sems):
    i = pl.program_id(0)
    cur = i & 1

    @pl.when(i == 0)
    def _():
        _row_copies(False, idx_ref, n_rows, src_hbm, functools.partial(dst_for_row, 0), sems.at[0])

    @pl.when(i + 1 < pl.num_programs(0))
    def _():
        _row_copies(False, next_idx_ref, n_rows, src_hbm, functools.partial(dst_for_row, 1 - cur), sems.at[1 - cur])

    _row_copies(True, idx_ref, n_rows, src_hbm, functools.partial(dst_for_row, cur), sems.at[cur])
    return cur


def _tile_rows(r):
    return pl.ds(pl.multiple_of(r * ROW_TILE, ROW_TILE), ROW_TILE)


def _dispatch_body(dest_ref, x_ref, xs_hbm, sem):
    def copy(e):
        src = x_ref.at[_tile_rows(lax.shift_right_logical(e, 1))]
        return pltpu.make_async_copy(src, xs_hbm.at[_tile_rows(dest_ref[0, 0, e])], sem)

    def trip(wait, g, c):
        for u in range(ROW_DMA_UNROLL):
            if wait:
                copy(g * ROW_DMA_UNROLL + u).wait()
            else:
                copy(g * ROW_DMA_UNROLL + u).start(priority=u % 2)
        return c

    n_trips = dest_ref.shape[2] // ROW_DMA_UNROLL
    lax.fori_loop(0, n_trips, functools.partial(trip, False), 0)
    lax.fori_loop(0, n_trips, functools.partial(trip, True), 0)


def _dispatch_rows(x, dest, pad_dest):
    assert TOP_K == 2
    tm = TM_COMBINE
    n_tokens = x.shape[0] // ROW_TILE
    token_steps = n_tokens // tm
    all_dest = jnp.concatenate([dest.reshape(-1), pad_dest]).reshape(-1, 1, TOP_K * tm)
    n_slots = all_dest.size
    return pl.pallas_call(
        _dispatch_body,
        grid=(all_dest.shape[0],),
        in_specs=[pl.BlockSpec((1, 1, TOP_K * tm), lambda i: (i, 0, 0), memory_space=pltpu.SMEM),
                  pl.BlockSpec((tm * ROW_TILE, LANES), lambda i: (jnp.minimum(i, token_steps - 1), 0))],
        out_specs=pl.BlockSpec(memory_space=pl.ANY),
        out_shape=jax.ShapeDtypeStruct((n_slots * ROW_TILE, LANES), F32),
        scratch_shapes=[pltpu.SemaphoreType.DMA(())],
        compiler_params=_params(1),
        name="moe_dispatch",
    )(all_dest, x)


def _stage_expert_weight(te_ref, w_hbm, layer, w_bf, stage, sems, chunk_of):
    t = pl.program_id(0)
    expert = te_ref[t]
    n_chunks = w_bf.size // stage[0].size
    n_bufs = stage.shape[0]

    @pl.when(jnp.logical_or(t == 0, expert != te_ref[jnp.maximum(t - 1, 0)]))
    def _():
        def copy(c):
            buf = c % n_bufs
            return pltpu.make_async_copy(chunk_of(w_hbm.at[layer, expert], c), stage.at[buf], sems.at[buf])

        for c in range(min(n_bufs - 1, n_chunks)):
            copy(c).start()
        for c in range(n_chunks):
            copy(c).wait()
            chunk_of(w_bf, c)[...] = stage[c % n_bufs].astype(w_bf.dtype)
            if c + n_bufs - 1 < n_chunks:
                copy(c + n_bufs - 1).start()


def _group_swiglu_body(te_ref, nu_ref, x_ref, w_hbm, o_ref, w_bf, stage, sems, *, layer):
    used = pl.program_id(0) < nu_ref[0]

    @pl.when(used)
    def _():
        _stage_expert_weight(te_ref, w_hbm, layer, w_bf, stage, sems,
                             lambda ref, c: ref.at[:, pl.ds(c * TN_FF, TN_FF)])
        _swiglu_cols(_from_row_tiles(x_ref, TM_GROUP).astype(BF16), w_bf, o_ref)

    @pl.when(jnp.logical_not(used))
    def _():
        o_ref[...] = jnp.zeros_like(o_ref)


def _group_down_body(te_ref, nu_ref, x_ref, w_hbm, o_ref, w_bf, stage, sems, *, layer):
    used = pl.program_id(0) < nu_ref[0]

    @pl.when(used)
    def _():
        rows = stage.shape[1]
        _stage_expert_weight(te_ref, w_hbm, layer, w_bf, stage, sems,
                             lambda ref, c: ref.at[pl.ds(c * rows, rows), :])
        _to_row_tiles(_dot(x_ref[...], w_bf[...]), o_ref)

    @pl.when(jnp.logical_not(used))
    def _():
        o_ref[...] = jnp.zeros_like(o_ref)


def _group_swiglu_up(xs, w_gate_up, layer, tile_expert, n_used):
    n_slots = xs.shape[0] // ROW_TILE
    k = ROW_TILE * LANES
    n_tiles = n_slots // TM_GROUP
    last = lambda t, nu: jnp.minimum(t, nu[0] - 1)
    grid_spec = pltpu.PrefetchScalarGridSpec(
        num_scalar_prefetch=2,
        grid=(n_tiles,),
        in_specs=[pl.BlockSpec((TM_GROUP * ROW_TILE, LANES), lambda t, te, nu: (last(t, nu), 0)),
                  pl.BlockSpec(memory_space=pl.ANY)],
        out_specs=pl.BlockSpec((TM_GROUP, D_FF), lambda t, te, nu: (t, 0)),
        scratch_shapes=[pltpu.VMEM((k, 2 * D_FF), BF16), pltpu.VMEM((WEIGHT_STAGE_BUFFERS, k, TN_FF), F32),
                        pltpu.SemaphoreType.DMA((WEIGHT_STAGE_BUFFERS,))],
    )
    return pl.pallas_call(
        functools.partial(_group_swiglu_body, layer=layer),
        grid_spec=grid_spec,
        out_shape=jax.ShapeDtypeStruct((n_slots, D_FF), BF16),
        compiler_params=_params(1),
        name="moe_swiglu_up",
    )(tile_expert, n_used, xs, w_gate_up)


def _group_down(act, w_down, layer, tile_expert, n_used):
    n_slots, k = act.shape
    n = w_down.shape[-1]
    n_tiles = n_slots // TM_GROUP
    last = lambda t, nu: jnp.minimum(t, nu[0] - 1)
    grid_spec = pltpu.PrefetchScalarGridSpec(
        num_scalar_prefetch=2,
        grid=(n_tiles,),
        in_specs=[pl.BlockSpec((TM_GROUP, k), lambda t, te, nu: (last(t, nu), 0)),
                  pl.BlockSpec(memory_space=pl.ANY)],
        out_specs=pl.BlockSpec((TM_GROUP * ROW_TILE, LANES), lambda t, te, nu: (t, 0)),
        scratch_shapes=[pltpu.VMEM((k, n), BF16), pltpu.VMEM((WEIGHT_STAGE_BUFFERS, TN_FF, n), F32),
                        pltpu.SemaphoreType.DMA((WEIGHT_STAGE_BUFFERS,))],
    )
    return pl.pallas_call(
        functools.partial(_group_down_body, layer=layer),
        grid_spec=grid_spec,
        out_shape=jax.ShapeDtypeStruct((n_slots * ROW_TILE, LANES), F32),
        compiler_params=_params(1),
        name="moe_down",
    )(tile_expert, n_used, act, w_down)


def _combined_rows(slot_ref, next_slot_ref, gate_ref, h_ref, y_hbm, buf, sems):
    cur = _prefetched_row_gather(slot_ref, next_slot_ref, TOP_K * TM_COMBINE, y_hbm,
                                 lambda b, e: buf.at[b, _tile_rows(e)], sems)
    gates = gate_ref[...]
    rows = TM_COMBINE * ROW_TILE
    best, second = [_from_row_tiles(buf.at[cur, c * rows:(c + 1) * rows], TM_COMBINE) for c in range(TOP_K)]
    return h_ref[...] + gates[:, 0:1] * best + gates[:, 1:2] * second


def _moe_combine_body(slot_ref, next_slot_ref, gate_ref, h_ref, y_hbm, out_ref, buf, sems):
    out_ref[...] = _combined_rows(slot_ref, next_slot_ref, gate_ref, h_ref, y_hbm, buf, sems)


def _moe_combine_norm_body(slot_ref, next_slot_ref, gate_ref, h_ref, g_ref, y_hbm, out_ref, buf, sems):
    rows = _combined_rows(slot_ref, next_slot_ref, gate_ref, h_ref, y_hbm, buf, sems)
    out_ref[...] = _rmsnorm_rows(rows, g_ref[...])


def _moe_combine(h, y_slots, slots, gates, g_final=None):
    t, d = h.shape
    tm = TM_COMBINE
    n_tiles = t // tm
    slot_idx = slots.reshape(n_tiles, tm, TOP_K).transpose(0, 2, 1).reshape(n_tiles, 1, TOP_K * tm)
    in_specs = [pl.BlockSpec((1, 1, TOP_K * tm), lambda i: (i, 0, 0), memory_space=pltpu.SMEM),
                pl.BlockSpec((1, 1, TOP_K * tm), lambda i: (jnp.minimum(i + 1, n_tiles - 1), 0, 0),
                             memory_space=pltpu.SMEM),
                pl.BlockSpec((tm, LANES), lambda i: (i, 0)),
                pl.BlockSpec((tm, d), lambda i: (i, 0))]
    args = [slot_idx, slot_idx, gates, h]
    if g_final is not None:
        in_specs.append(pl.BlockSpec((1, d), lambda i: (0, 0)))
        args.append(g_final.reshape(1, d))
    return pl.pallas_call(
        _moe_combine_body if g_final is None else _moe_combine_norm_body,
        grid=(n_tiles,),
        in_specs=in_specs + [pl.BlockSpec(memory_space=pl.ANY)],
        out_specs=pl.BlockSpec((tm, d), lambda i: (i, 0)),
        out_shape=jax.ShapeDtypeStruct((t, d), F32),
        scratch_shapes=[pltpu.VMEM((2, TOP_K * tm * ROW_TILE, LANES), F32), pltpu.SemaphoreType.DMA((2,))],
        compiler_params=_params(1),
        name="moe_combine",
    )(*args, y_slots)


def _moe_layer(h, hn, gates, idx, w_gate_up, w_down, layer, g_final=None):
    t = h.shape[0]
    n_slots = t * TOP_K + N_EXPERTS * TM_GROUP
    slots, pad_slots, tile_expert, n_used = _route_layout(idx, n_slots)
    xs = _dispatch_rows(hn, slots, pad_slots)
    act = _group_swiglu_up(xs, w_gate_up, layer, tile_expert, n_used)
    y_slots = _group_down(act, w_down, layer, tile_expert, n_used)
    return _moe_combine(h, y_slots, slots, gates, g_final)


def kernel(x, mix_norm, ffn_norm, dil_w_in, dil_w_out, sb_w_qkv, sb_w_out, ffn_w_gate_up, ffn_w_down,
           moe_w_router, moe_w_gate_up, moe_w_down, final_norm):
    batch, seq, d = x.shape
    depth = mix_norm.shape[0]
    h = x.reshape(batch * seq, d)
    hn = None
    for i in range(depth):
        j = i // 2
        last = i + 1 == depth
        if i % 2 == 0:
            h, hn = _dilated_mixer(h, mix_norm[i], dil_w_in[j], dil_w_out[j], ffn_norm[i], batch, seq)
            act = _swiglu_up(hn, ffn_w_gate_up[j].astype(BF16))
            if last:
                h, hn = _matmul_residual(act, ffn_w_down[j].astype(BF16), h, "ffn_down"), None
            else:
                h, hn = _matmul_residual(act, ffn_w_down[j].astype(BF16), h, "ffn_down", g_next=mix_norm[i + 1])
        else:
            if hn is None:
                hn = _rmsnorm(h, mix_norm[i], BF16)
            qkv = _matmul(hn, sb_w_qkv[j].astype(BF16), BF16, "sb_qkv_proj")
            y = _stick_breaking_attention(qkv, batch, seq)
            h, hn_tiles, gates, idx = _out_proj_router(y, sb_w_out[j].astype(BF16), h, ffn_norm[i], moe_w_router[j])
            h = _moe_layer(h, hn_tiles, gates, idx, moe_w_gate_up, moe_w_down, j,
                           g_final=final_norm if last else None)
            hn = None
            if last:
                return h.reshape(batch, seq, d)
    return _rmsnorm(h, final_norm, F32).reshape(batch, seq, d)
```

```python
import functools

import numpy as np
import jax
import jax.numpy as jnp
from jax import lax
from jax.experimental import pallas as pl
from jax.experimental.pallas import tpu as pltpu

D_MODEL = 1024
N_HEADS = 16
HEAD_DIM = 64
D_ATTN = N_HEADS * HEAD_DIM
DILATED_GROUPS = ((128, 1), (512, 4), (2048, 16))
N_GROUPS = len(DILATED_GROUPS)
BLOCK = 128
D_FF = 3584
N_EXPERTS = 8
TOP_K = 2
RMS_EPS = 1e-6
ATTN_SCALE = 0.125
NEG_LOG2E = -float(np.float32(np.log2(np.e)))

LANES = 128
VMEM_LIMIT_BYTES = 56 * 1024 * 1024

TM = 1024
TN = 1024
TN_FF = 512
TM_GROUP = 512
TM_COMBINE = 256
TM_DISPATCH = 512
TM_EPILOGUE = 256
SB_CHUNK = 256
SB_HALVES = 2
SB_DEAD_CARRY = -104.0
DIL_PAIRS_PER_STAGE = 4
ROW_DMA_UNROLL = 8
ROW_TILE = D_MODEL // LANES
WEIGHT_STAGE_BUFFERS = 4

F32 = jnp.float32
BF16 = jnp.bfloat16

ALIBI_SLOPES = tuple(float(np.float32(2.0 ** (-8.0 * (h + 1) / N_HEADS))) for h in range(N_HEADS))


def _params(n_axes):
    return pltpu.CompilerParams(dimension_semantics=("arbitrary",) * n_axes, vmem_limit_bytes=VMEM_LIMIT_BYTES)


def _dot(a, b):
    return jnp.dot(a, b, preferred_element_type=F32)


def _dot_nt(a, b):
    return lax.dot_general(a, b, (((1,), (1,)), ((), ())), preferred_element_type=F32)


def _split_bf16(x):
    hi = x.astype(BF16)
    lo = (x - hi.astype(F32)).astype(BF16)
    return hi, lo


def _rmsnorm_rows(x, g):
    return x * lax.rsqrt(jnp.mean(x * x, axis=-1, keepdims=True) + RMS_EPS) * g


def _rmsnorm_body(x_ref, g_ref, o_ref):
    o_ref[...] = _rmsnorm_rows(x_ref[...], g_ref[...]).astype(o_ref.dtype)


def _rmsnorm(x, g, out_dtype):
    t, d = x.shape
    return pl.pallas_call(
        _rmsnorm_body,
        grid=(t // TM,),
        in_specs=[pl.BlockSpec((TM, d), lambda i: (i, 0)), pl.BlockSpec((1, d), lambda i: (0, 0))],
        out_specs=pl.BlockSpec((TM, d), lambda i: (i, 0)),
        out_shape=jax.ShapeDtypeStruct((t, d), out_dtype),
        compiler_params=_params(1),
        name="rmsnorm",
    )(x, g.reshape(1, d))


def _rmsnorm_residue_body(x_ref, g_ref, *refs):
    o_refs, lanes_ref = refs[:-1], refs[-1]
    y = _rmsnorm_rows(x_ref[0], g_ref[...])
    n_chunks = y.shape[1] // LANES
    for c in range(n_chunks):
        lanes_ref[c] = y[:, c * LANES:(c + 1) * LANES]
    for o_ref, (_, dilation) in zip(o_refs, DILATED_GROUPS):
        if dilation == 1:
            o_ref[0, 0] = y.astype(o_ref.dtype)
            continue
        rows = TM // dilation
        for r in range(dilation):
            for c in range(n_chunks):
                picked = lanes_ref[c, pl.ds(r, rows, stride=dilation), :]
                o_ref[0, r, :, c * LANES:(c + 1) * LANES] = picked.astype(o_ref.dtype)


def _rmsnorm_by_residue(x, g, batch, seq):
    d_model = x.shape[1]
    tiles = seq // TM
    out_specs, out_shape = [], []
    for _, dilation in DILATED_GROUPS:
        out_specs.append(pl.BlockSpec((1, dilation, TM // dilation, d_model), lambda b, i: (b, 0, i, 0)))
        out_shape.append(jax.ShapeDtypeStruct((batch, dilation, seq // dilation, d_model), BF16))
    outs = pl.pallas_call(
        _rmsnorm_residue_body,
        grid=(batch, tiles),
        in_specs=[pl.BlockSpec((1, TM, d_model), lambda b, i: (b, i, 0)),
                  pl.BlockSpec((1, d_model), lambda b, i: (0, 0))],
        out_specs=out_specs,
        out_shape=out_shape,
        scratch_shapes=[pltpu.VMEM((d_model // LANES, TM, LANES), F32)],
        compiler_params=_params(2),
        name="rmsnorm_by_residue",
    )(x.reshape(batch, seq, d_model), g.reshape(1, d_model))
    return [o.reshape(batch * seq, d_model) for o in outs]


def _mm_body(x_ref, w_ref, o_ref):
    x = x_ref[...]
    for c in range(0, o_ref.shape[1], TN):
        o_ref[:, c:c + TN] = _dot(x, w_ref[:, c:c + TN]).astype(o_ref.dtype)


def _mm_res_body(x_ref, w_ref, r_ref, o_ref):
    o_ref[...] = r_ref[...] + _dot(x_ref[...], w_ref[...])


def _mm_res_norm_body(x_ref, w_ref, r_ref, g_ref, o_ref, hn_ref):
    h_new = r_ref[...] + _dot(x_ref[...], w_ref[...])
    o_ref[...] = h_new
    hn_ref[...] = _rmsnorm_rows(h_new, g_ref[...]).astype(hn_ref.dtype)


def _matmul(x, w, out_dtype, name, first_col=0, n_cols=None):
    m, k = x.shape
    n = w.shape[1] if n_cols is None else n_cols
    panel = first_col // n
    return pl.pallas_call(
        _mm_body,
        grid=(m // TM,),
        in_specs=[pl.BlockSpec((TM, k), lambda i: (i, 0)), pl.BlockSpec((k, n), lambda i: (0, panel))],
        out_specs=pl.BlockSpec((TM, n), lambda i: (i, 0)),
        out_shape=jax.ShapeDtypeStruct((m, n), out_dtype),
        compiler_params=_params(1),
        name=name,
    )(x, w)


def _matmul_residual(x, w, res, name, g_next=None):
    m, k = x.shape
    n = w.shape[1]
    row = pl.BlockSpec((TM, n), lambda i: (i, 0))
    in_specs = [pl.BlockSpec((TM, k), lambda i: (i, 0)), pl.BlockSpec((k, n), lambda i: (0, 0)), row]
    if g_next is None:
        return pl.pallas_call(
            _mm_res_body,
            grid=(m // TM,),
            in_specs=in_specs,
            out_specs=row,
            out_shape=jax.ShapeDtypeStruct((m, n), F32),
            compiler_params=_params(1),
            name=name,
        )(x, w, res)
    return pl.pallas_call(
        _mm_res_norm_body,
        grid=(m // TM,),
        in_specs=in_specs + [pl.BlockSpec((1, n), lambda i: (0, 0))],
        out_specs=[row, row],
        out_shape=[jax.ShapeDtypeStruct((m, n), F32), jax.ShapeDtypeStruct((m, n), BF16)],
        compiler_params=_params(1),
        name=name,
    )(x, w, res, g_next.reshape(1, n))


def _swiglu(g, u):
    return g * (1.0 / (1.0 + jnp.exp(-g))) * u


def _swiglu_cols(x, w_gate_up, o_ref):
    for c in range(0, D_FF, TN_FF):
        gate = _dot(x, w_gate_up[:, c:c + TN_FF])
        up = _dot(x, w_gate_up[:, D_FF + c:D_FF + c + TN_FF])
        o_ref[:, c:c + TN_FF] = _swiglu(gate, up).astype(o_ref.dtype)


def _swiglu_body(x_ref, w_ref, o_ref):
    _swiglu_cols(x_ref[...], w_ref, o_ref)


def _swiglu_up(x, w_gate_up):
    m, k = x.shape
    return pl.pallas_call(
        _swiglu_body,
        grid=(m // TM,),
        in_specs=[pl.BlockSpec((TM, k), lambda i: (i, 0)),
                  pl.BlockSpec((k, 2 * D_FF), lambda i: (0, 0))],
        out_specs=pl.BlockSpec((TM, D_FF), lambda i: (i, 0)),
        out_shape=jax.ShapeDtypeStruct((m, D_FF), BF16),
        compiler_params=_params(1),
        name="swiglu_up",
    )(x, w_gate_up)


def _alibi_window_bias(dilation):
    a = np.arange(BLOCK)[:, None]
    m = np.arange(2 * BLOCK)[None, :]
    rel = BLOCK + a - m
    in_window = (rel >= 0) & (rel <= BLOCK)
    slopes = np.asarray(ALIBI_SLOPES, np.float32)[:, None, None]
    bias = -slopes * (rel * dilation).astype(np.float32)[None]
    later = np.where(in_window[None], bias, -np.inf)
    first = np.where((m >= BLOCK)[None], later, -np.inf)
    return np.stack([first, later]).astype(np.float32)


def _dil_attn_body(q_ref, kc_ref, kp_ref, vc_ref, vp_ref, bias_ref, o_ref, lse_ref):
    n_keys = 2 * BLOCK
    lane = lax.broadcasted_iota(jnp.int32, (BLOCK, LANES), 1)
    first_head = lane < HEAD_DIM
    first_head_keys = lax.broadcasted_iota(jnp.int32, (n_keys, LANES), 1) < HEAD_DIM
    lse_tile = jnp.zeros((BLOCK, LANES), F32)

    def by_head(pair_rows):
        zero = jnp.zeros_like(pair_rows)
        return jnp.concatenate([jnp.where(first_head_keys, pair_rows, zero),
                                jnp.where(first_head_keys, zero, pair_rows)], axis=0)

    for first_pair in range(0, N_HEADS // 2, DIL_PAIRS_PER_STAGE):
        pairs = range(first_pair, first_pair + DIL_PAIRS_PER_STAGE)
        cols = {p: slice(p * LANES, (p + 1) * LANES) for p in pairs}
        scores = {p: _dot_nt(q_ref[:, cols[p]] * ATTN_SCALE,
                             by_head(jnp.concatenate([kp_ref[:, cols[p]], kc_ref[:, cols[p]]], axis=0)))
                  for p in pairs}
        probs, norms = {}, {}
        for p in pairs:
            pair_probs = []
            for hh in range(2):
                h = 2 * p + hh
                s = scores[p][:, hh * n_keys:(hh + 1) * n_keys] + bias_ref[0, h]
                s_max = jnp.max(s, axis=1, keepdims=True)
                e = jnp.exp(s - s_max)
                z = jnp.sum(e, axis=1, keepdims=True)
                lse_tile = jnp.where(lane == h, s_max + jnp.log(z), lse_tile)
                pair_probs.append(e.astype(BF16))
                norms[h] = z
            probs[p] = jnp.concatenate(pair_probs, axis=1)
        for p in pairs:
            v_heads = by_head(jnp.concatenate([vp_ref[:, cols[p]], vc_ref[:, cols[p]]], axis=0))
            o_pair = _dot(probs[p], v_heads) / jnp.where(first_head, norms[2 * p], norms[2 * p + 1])
            o_ref[:, cols[p]] = o_pair.astype(o_ref.dtype)
    lse_ref[...] = lse_tile


def _dilated_attention(proj, batch, seq, group):
    _, dilation = DILATED_GROUPS[group]
    n_blocks = seq // dilation // BLOCK

    def block(which, back):
        return lambda b, r, n: ((b * dilation + r) * n_blocks + jnp.maximum(n - back, 0), which)

    blk = (BLOCK, D_ATTN)
    bias = jnp.asarray(_alibi_window_bias(dilation))
    return pl.pallas_call(
        _dil_attn_body,
        grid=(batch, dilation, n_blocks),
        in_specs=[pl.BlockSpec(blk, block(0, 0)), pl.BlockSpec(blk, block(1, 0)), pl.BlockSpec(blk, block(1, 1)),
                  pl.BlockSpec(blk, block(2, 0)), pl.BlockSpec(blk, block(2, 1)),
                  pl.BlockSpec((1,) + bias.shape[1:], lambda b, r, n: (jnp.minimum(n, 1), 0, 0, 0))],
        out_specs=[pl.BlockSpec(blk, block(0, 0)), pl.BlockSpec((BLOCK, LANES), block(0, 0))],
        out_shape=[jax.ShapeDtypeStruct((batch * seq, D_ATTN), BF16),
                   jax.ShapeDtypeStruct((batch * seq, LANES), F32)],
        compiler_params=_params(3),
        name=f"dilated_attn_g{group}",
    )(proj, proj, proj, proj, proj, bias)


def _dil_out_body(o0_ref, o1_ref, o2_ref, l0_ref, l1_ref, l2_ref, e_ref, w_ref, h_ref, g_ref, out_ref, hn_ref,
                  rows_ref, stat_ref):
    tile = h_ref.shape[1]
    n_chunks = D_ATTN // LANES
    outs, lses = [], []
    for grp, (o_ref, l_ref) in enumerate(zip((o0_ref, o1_ref, o2_ref), (l0_ref, l1_ref, l2_ref))):
        dilation = DILATED_GROUPS[grp][1]
        if dilation == 1:
            outs.append(o_ref[0, 0].astype(F32))
            lses.append(l_ref[0, 0])
            continue
        for r in range(dilation):
            rows = pl.ds(r, tile // dilation, stride=dilation)
            stat_ref[grp, rows, :] = l_ref[0, r]
            for c in range(n_chunks):
                rows_ref[grp, c, rows, :] = o_ref[0, r, :, c * LANES:(c + 1) * LANES].astype(F32)
        outs.append(jnp.concatenate([rows_ref[grp, c] for c in range(n_chunks)], axis=1))
        lses.append(stat_ref[grp])
    top = jnp.maximum(jnp.maximum(lses[0], lses[1]), lses[2])
    es = [jnp.exp(l - top) for l in lses]
    den = es[0] + es[1] + es[2]
    expand = e_ref[...]
    y = None
    for grp in range(N_GROUPS):
        term = _dot((es[grp] / den).astype(BF16), expand) * outs[grp]
        y = term if y is None else y + term
    h_new = h_ref[0] + _dot(y.astype(BF16), w_ref[...])
    out_ref[0] = h_new
    hn_ref[0] = _rmsnorm_rows(h_new, g_ref[...]).astype(hn_ref.dtype)


def _dilated_out_proj(outs, lses, w_out, h, g_next, batch, seq):
    tm = TM_EPILOGUE
    expand = (np.arange(LANES)[:, None] == (np.arange(D_ATTN)[None, :] // HEAD_DIM)).astype(np.float32)

    def by_residue(width, dilation):
        return pl.BlockSpec((1, dilation, tm // dilation, width), lambda b, i: (b, 0, i, 0))

    def grouped(arrays, width):
        return [a.reshape(batch, d, seq // d, width) for a, (_, d) in zip(arrays, DILATED_GROUPS)]

    row = pl.BlockSpec((1, tm, D_MODEL), lambda b, i: (b, i, 0))
    full = lambda r, c: pl.BlockSpec((r, c), lambda b, i: (0, 0))
    h_new, hn = pl.pallas_call(
        _dil_out_body,
        grid=(batch, seq // tm),
        in_specs=[by_residue(D_ATTN, d) for _, d in DILATED_GROUPS]
        + [by_residue(LANES, d) for _, d in DILATED_GROUPS]
        + [full(LANES, D_ATTN), full(D_ATTN, D_MODEL), row, full(1, D_MODEL)],
        out_specs=[row, row],
        out_shape=[jax.ShapeDtypeStruct((batch, seq, D_MODEL), F32),
                   jax.ShapeDtypeStruct((batch, seq, D_MODEL), BF16)],
        scratch_shapes=[pltpu.VMEM((N_GROUPS, D_ATTN // LANES, tm, LANES), F32),
                        pltpu.VMEM((N_GROUPS, tm, LANES), F32)],
        compiler_params=_params(2),
        name="dilated_out_proj",
    )(*grouped(outs, D_ATTN), *grouped(lses, LANES), jnp.asarray(expand, BF16), w_out,
      h.reshape(batch, seq, D_MODEL), g_next.reshape(1, D_MODEL))
    return h_new.reshape(batch * seq, D_MODEL), hn.reshape(batch * seq, D_MODEL)


def _dilated_mixer(h, g, w_in, w_out, g_next, batch, seq):
    w_in = w_in.astype(BF16)
    parts = []
    for group, hn in enumerate(_rmsnorm_by_residue(h, g, batch, seq)):
        proj = _matmul(hn, w_in, BF16, f"dilated_in_proj_g{group}", first_col=group * 3 * D_ATTN, n_cols=3 * D_ATTN)
        parts.append(_dilated_attention(proj, batch, seq, group))
    return _dilated_out_proj([p[0] for p in parts], [p[1] for p in parts], w_out.astype(BF16), h, g_next,
                             batch, seq)


def _sb_attn_body(q_ref, k_ref, v_ref, tri_ref, o_ref):
    i = pl.program_id(2)
    row = lax.broadcasted_iota(jnp.int32, (SB_CHUNK, SB_CHUNK), 0)
    col = lax.broadcasted_iota(jnp.int32, (SB_CHUNK, SB_CHUNK), 1)
    strict = col < row
    first_head = lax.broadcasted_iota(jnp.int32, (SB_CHUNK, LANES), 1) < HEAD_DIM
    head_lanes = (first_head, ~first_head)
    tri = tri_ref[...]
    heads = range(2)
    q_heads = []
    for half in range(SB_HALVES):
        q_pair = q_ref[0, half * SB_CHUNK:(half + 1) * SB_CHUNK, :] * ATTN_SCALE
        q_heads.append([jnp.where(sel, q_pair, jnp.zeros_like(q_pair)) for sel in head_lanes])

    def advance(state, work):
        def rows(ref, j):
            return ref[0, pl.ds(pl.multiple_of(j * SB_CHUNK, SB_CHUNK), SB_CHUNK), :]

        chunks = list({id(item[0]): item[0] for item in work}.values())
        k_of = {id(j): rows(k_ref, j) for j in chunks}
        v_of = {id(j): [jnp.where(sel, v, jnp.zeros_like(v)) for sel in head_lanes]
                for j in chunks for v in [rows(v_ref, j)]}
        chains = [(j, half, hh, diag) for j, half, diag in work for hh in heads]
        zs = [_dot_nt(q_heads[half][hh], k_of[id(j)]) for j, half, hh, _ in chains]
        zs = [jnp.where(strict, z, -1e30) if diag else z for z, (_, _, _, diag) in zip(zs, chains)]
        sps = [jnp.maximum(z, 0.0) + jnp.log(1.0 + jnp.exp2(jnp.abs(z) * NEG_LOG2E)) for z in zs]
        later = [_dot(sp.astype(BF16), tri) for sp in sps]
        new_state = list(state)
        for c, (j, half, hh, _) in enumerate(chains):
            carries, acc = new_state[half]
            att = jnp.exp((zs[c] - sps[c]) + later[c] + carries[hh]).astype(BF16)
            chunk_sum = later[c][:, 0:1] - sps[c][:, 0:1]
            carries = tuple(carries[x] + chunk_sum if x == hh else carries[x] for x in heads)
            new_state[half] = (carries, acc + _dot(att, v_of[id(j)][hh]))
        return tuple(new_state)

    def max_carry(state, half):
        return functools.reduce(jnp.maximum, [jnp.max(c) for c in state[half][0]])

    zero_carry = jnp.zeros((SB_CHUNK, 1), F32)
    state = (((zero_carry, zero_carry), jnp.zeros((SB_CHUNK, LANES), F32)),) * SB_HALVES
    diag1, diag0 = SB_HALVES * i + 1, SB_HALVES * i
    state = advance(state, ((diag1, 1, True), (diag0, 0, True), (diag0, 1, False)))

    def walk(halves, watched, start):
        def live(loop):
            jj, _, carry_bound = loop
            return jnp.logical_and(jj < SB_HALVES * i, carry_bound > SB_DEAD_CARRY)

        def step(loop):
            jj, state, _ = loop
            j = SB_HALVES * i - 1 - jj
            state = advance(state, tuple((j, half, False) for half in halves))
            return jj + 1, state, max_carry(state, watched)

        jj, state = start
        jj, state, _ = lax.while_loop(live, step, (jj, state, max_carry(state, watched)))
        return jj, state

    _, state = walk((0,), 0, walk((0, 1), 1, (jnp.int32(0), state)))
    for half in range(SB_HALVES):
        o_ref[0, half * SB_CHUNK:(half + 1) * SB_CHUNK, :] = state[half][1].astype(o_ref.dtype)


def _stick_breaking_attention(qkv, batch, seq):
    n_pairs = N_HEADS // 2
    view = qkv.reshape(batch, seq, 3 * D_ATTN)
    q_rows = SB_HALVES * SB_CHUNK
    j = np.arange(SB_CHUNK)
    tri = -(j[:, None] > j[None, :]).astype(np.float32)
    out = pl.pallas_call(
        _sb_attn_body,
        grid=(batch, n_pairs, seq // q_rows),
        in_specs=[pl.BlockSpec((1, q_rows, LANES), lambda b, p, i: (b, i, p)),
                  pl.BlockSpec((1, seq, LANES), lambda b, p, i: (b, 0, n_pairs + p)),
                  pl.BlockSpec((1, seq, LANES), lambda b, p, i: (b, 0, 2 * n_pairs + p)),
                  pl.BlockSpec((SB_CHUNK, SB_CHUNK), lambda b, p, i: (0, 0))],
        out_specs=pl.BlockSpec((1, q_rows, LANES), lambda b, p, i: (b, i, p)),
        out_shape=jax.ShapeDtypeStruct((batch, seq, D_ATTN), BF16),
        compiler_params=_params(3),
        name="stick_breaking_attn",
    )(view, view, view, jnp.asarray(tri, BF16))
    return out.reshape(batch * seq, D_ATTN)


def _out_proj_router_body(x_ref, w_ref, r_ref, g_ref, wh_ref, wl_ref, out_ref, hn_ref, gate_ref, idx_ref):
    h_new = r_ref[...] + _dot(x_ref[...], w_ref[...])
    out_ref[...] = h_new
    y = _rmsnorm_rows(h_new, g_ref[...])
    _to_row_tiles(y, hn_ref)
    y_hi, y_lo = _split_bf16(y)
    w_hi, w_lo = wh_ref[...], wl_ref[...]
    logits = _dot(y_hi, w_hi) + _dot(y_hi, w_lo) + _dot(y_lo, w_hi)
    lane = lax.broadcasted_iota(jnp.int32, logits.shape, 1)
    neg_inf = F32(-jnp.inf)
    logits = jnp.where(lane < N_EXPERTS, logits, neg_inf)
    v1 = jnp.max(logits, axis=1, keepdims=True)
    i1 = jnp.min(jnp.where(logits == v1, lane, LANES), axis=1, keepdims=True)
    rest = jnp.where(lane == i1, neg_inf, logits)
    v2 = jnp.max(rest, axis=1, keepdims=True)
    i2 = jnp.min(jnp.where(rest == v2, lane, LANES), axis=1, keepdims=True)
    e2 = jnp.exp(v2 - v1)
    w1 = 1.0 / (1.0 + e2)
    w2 = e2 * w1
    gate_ref[...] = jnp.where(lane == 0, w1, jnp.where(lane == 1, w2, 0.0))
    idx_ref[...] = jnp.where(lane == 0, i1, jnp.where(lane == 1, i2, 0))


def _out_proj_router(x, w_out, h, g, w_router):
    t, d = h.shape
    k = x.shape[1]
    tm = TM_EPILOGUE
    w_pad = jnp.zeros((d, LANES), F32).at[:, :N_EXPERTS].set(w_router)
    w_hi = w_pad.astype(BF16)
    w_lo = (w_pad - w_hi.astype(F32)).astype(BF16)
    row = lambda w: pl.BlockSpec((tm, w), lambda i: (i, 0))
    full = lambda r, c: pl.BlockSpec((r, c), lambda i: (0, 0))
    return pl.pallas_call(
        _out_proj_router_body,
        grid=(t // tm,),
        in_specs=[row(k), full(k, d), row(d), full(1, d), full(d, LANES), full(d, LANES)],
        out_specs=[row(d), pl.BlockSpec((tm * ROW_TILE, LANES), lambda i: (i, 0)), row(LANES), row(LANES)],
        out_shape=[jax.ShapeDtypeStruct((t, d), F32), jax.ShapeDtypeStruct((t * ROW_TILE, LANES), F32),
                   jax.ShapeDtypeStruct((t, LANES), F32), jax.ShapeDtypeStruct((t, LANES), jnp.int32)],
        compiler_params=_params(1),
        name="sb_out_proj_router",
    )(x, w_out, h, g.reshape(1, d), w_hi, w_lo)


def _route_layout(idx, n_slots):
    n_tiles = n_slots // TM_GROUP
    flat = idx[:, :TOP_K].reshape(-1)
    onehot = (flat[:, None] == jnp.arange(N_EXPERTS, dtype=jnp.int32)[None, :]).astype(jnp.int32)
    csum = jnp.cumsum(onehot, axis=0)
    rank = jnp.sum(csum * onehot, axis=1) - 1
    counts = csum[-1]
    padded = ((counts + TM_GROUP - 1) // TM_GROUP) * TM_GROUP
    pad_end = jnp.cumsum(padded)
    group_start = pad_end - padded
    dest = group_start[flat] + rank
    pad_count = padded - counts
    pad_cum = jnp.cumsum(pad_count)
    k = jnp.arange(n_slots - flat.shape[0], dtype=jnp.int32)
    e_k = jnp.minimum(jnp.sum((k[:, None] >= pad_cum[None, :]).astype(jnp.int32), axis=1), N_EXPERTS - 1)
    in_group = (group_start + counts)[e_k] + k - (pad_cum - pad_count)[e_k]
    pad_slots = jnp.where(k < pad_cum[-1], in_group, pad_end[-1] + k - pad_cum[-1])
    tile_start = jnp.arange(n_tiles, dtype=jnp.int32) * TM_GROUP
    tile_expert = jnp.minimum(jnp.sum((tile_start[:, None] >= pad_end[None, :]).astype(jnp.int32), axis=1),
                              N_EXPERTS - 1)
    n_used = (pad_end[-1:] // TM_GROUP).astype(jnp.int32)
    return dest.reshape(-1, TOP_K), pad_slots, tile_expert, n_used


def _to_row_tiles(rows, out_ref):
    n, d = rows.shape
    for c in range(d // LANES):
        out_ref[pl.ds(c, n, stride=d // LANES), :] = rows[:, c * LANES:(c + 1) * LANES]


def _from_row_tiles(tiles_ref, n):
    return jnp.concatenate([tiles_ref[pl.ds(c, n, stride=ROW_TILE), :] for c in range(ROW_TILE)], axis=1)


def _row_copies(wait, idx_ref, n_rows, src_hbm, dst_for_row, sem):
    def copy(r):
        src = src_hbm.at[pl.ds(pl.multiple_of(idx_ref[0, 0, r] * ROW_TILE, ROW_TILE), ROW_TILE)]
        return pltpu.make_async_copy(src, dst_for_row(r), sem)

    def trip(g, c):
        for u in range(ROW_DMA_UNROLL):
            if wait:
                copy(g * ROW_DMA_UNROLL + u).wait()
            else:
                copy(g * ROW_DMA_UNROLL + u).start(priority=u % 2)
        return c

    lax.fori_loop(0, n_rows // ROW_DMA_UNROLL, trip, 0)


def _prefetched_row_gather(idx_ref, next_idx_ref, n_rows, src_hbm, dst_for_row, sems):
    i = pl.program_id(0)
    cur = i & 1

    @pl.when(i == 0)
    def _():
        _row_copies(False, idx_ref, n_rows, src_hbm, functools.partial(dst_for_row, 0), sems.at[0])

    @pl.when(i + 1 < pl.num_programs(0))
    def _():
        _row_copies(False, next_idx_ref, n_rows, src_hbm, functools.partial(dst_for_row, 1 - cur), sems.at[1 - cur])

    _row_copies(True, idx_ref, n_rows, src_hbm, functools.partial(dst_for_row, cur), sems.at[cur])
    return cur


def _tile_rows(r):
    return pl.ds(pl.multiple_of(r * ROW_TILE, ROW_TILE), ROW_TILE)


def _dispatch_body(dest_ref, x_ref, xs_hbm, sem):
    def copy(e):
        src = x_ref.at[_tile_rows(lax.shift_right_logical(e, 1))]
        return pltpu.make_async_copy(src, xs_hbm.at[_tile_rows(dest_ref[0, 0, e])], sem)

    def trip(wait, g, c):
        for u in range(ROW_DMA_UNROLL):
            if wait:
                copy(g * ROW_DMA_UNROLL + u).wait()
            else:
                copy(g * ROW_DMA_UNROLL + u).start(priority=u % 2)
        return c

    n_trips = dest_ref.shape[2] // ROW_DMA_UNROLL
    lax.fori_loop(0, n_trips, functools.partial(trip, False), 0)
    lax.fori_loop(0, n_trips, functools.partial(trip, True), 0)


def _dispatch_rows(x, dest, pad_dest):
    assert TOP_K == 2
    tm = TM_DISPATCH
    n_tokens = x.shape[0] // ROW_TILE
    token_steps = n_tokens // tm
    all_dest = jnp.concatenate([dest.reshape(-1), pad_dest]).reshape(-1, 1, TOP_K * tm)
    n_slots = all_dest.size
    return pl.pallas_call(
        _dispatch_body,
        grid=(all_dest.shape[0],),
        in_specs=[pl.BlockSpec((1, 1, TOP_K * tm), lambda i: (i, 0, 0), memory_space=pltpu.SMEM),
                  pl.BlockSpec((tm * ROW_TILE, LANES), lambda i: (jnp.minimum(i, token_steps - 1), 0))],
        out_specs=pl.BlockSpec(memory_space=pl.ANY),
        out_shape=jax.ShapeDtypeStruct((n_slots * ROW_TILE, LANES), F32),
        scratch_shapes=[pltpu.SemaphoreType.DMA(())],
        compiler_params=_params(1),
        name="moe_dispatch",
    )(all_dest, x)


def _stage_expert_weight(te_ref, w_hbm, layer, w_bf, stage, sems, chunk_of):
    t = pl.program_id(0)
    expert = te_ref[t]
    n_chunks = w_bf.size // stage[0].size
    n_bufs = stage.shape[0]

    @pl.when(jnp.logical_or(t == 0, expert != te_ref[jnp.maximum(t - 1, 0)]))
    def _():
        def copy(c):
            buf = c % n_bufs
            return pltpu.make_async_copy(chunk_of(w_hbm.at[layer, expert], c), stage.at[buf], sems.at[buf])

        for c in range(min(n_bufs - 1, n_chunks)):
            copy(c).start()
        for c in range(n_chunks):
            copy(c).wait()
            chunk_of(w_bf, c)[...] = stage[c % n_bufs].astype(w_bf.dtype)
            if c + n_bufs - 1 < n_chunks:
                copy(c + n_bufs - 1).start()


def _group_swiglu_body(te_ref, nu_ref, x_ref, w_hbm, o_ref, w_bf, stage, sems, *, layer):
    used = pl.program_id(0) < nu_ref[0]

    @pl.when(used)
    def _():
        _stage_expert_weight(te_ref, w_hbm, layer, w_bf, stage, sems,
                             lambda ref, c: ref.at[:, pl.ds(c * TN_FF, TN_FF)])
        _swiglu_cols(_from_row_tiles(x_ref, TM_GROUP).astype(BF16), w_bf, o_ref)

    @pl.when(jnp.logical_not(used))
    def _():
        o_ref[...] = jnp.zeros_like(o_ref)


def _group_down_body(te_ref, nu_ref, x_ref, w_hbm, o_ref, w_bf, stage, sems, *, layer):
    used = pl.program_id(0) < nu_ref[0]

    @pl.when(used)
    def _():
        rows = stage.shape[1]
        _stage_expert_weight(te_ref, w_hbm, layer, w_bf, stage, sems,
                             lambda ref, c: ref.at[pl.ds(c * rows, rows), :])
        _to_row_tiles(_dot(x_ref[...], w_bf[...]), o_ref)

    @pl.when(jnp.logical_not(used))
    def _():
        o_ref[...] = jnp.zeros_like(o_ref)


def _group_swiglu_up(xs, w_gate_up, layer, tile_expert, n_used):
    n_slots = xs.shape[0] // ROW_TILE
    k = ROW_TILE * LANES
    n_tiles = n_slots // TM_GROUP
    last = lambda t, nu: jnp.minimum(t, nu[0] - 1)
    grid_spec = pltpu.PrefetchScalarGridSpec(
        num_scalar_prefetch=2,
        grid=(n_tiles,),
        in_specs=[pl.BlockSpec((TM_GROUP * ROW_TILE, LANES), lambda t, te, nu: (last(t, nu), 0)),
                  pl.BlockSpec(memory_space=pl.ANY)],
        out_specs=pl.BlockSpec((TM_GROUP, D_FF), lambda t, te, nu: (t, 0)),
        scratch_shapes=[pltpu.VMEM((k, 2 * D_FF), BF16), pltpu.VMEM((WEIGHT_STAGE_BUFFERS, k, TN_FF), F32),
                        pltpu.SemaphoreType.DMA((WEIGHT_STAGE_BUFFERS,))],
    )
    return pl.pallas_call(
        functools.partial(_group_swiglu_body, layer=layer),
        grid_spec=grid_spec,
        out_shape=jax.ShapeDtypeStruct((n_slots, D_FF), BF16),
        compiler_params=_params(1),
        name="moe_swiglu_up",
    )(tile_expert, n_used, xs, w_gate_up)


def _group_down(act, w_down, layer, tile_expert, n_used):
    n_slots, k = act.shape
    n = w_down.shape[-1]
    n_tiles = n_slots // TM_GROUP
    last = lambda t, nu: jnp.minimum(t, nu[0] - 1)
    grid_spec = pltpu.PrefetchScalarGridSpec(
        num_scalar_prefetch=2,
        grid=(n_tiles,),
        in_specs=[pl.BlockSpec((TM_GROUP, k), lambda t, te, nu: (last(t, nu), 0)),
                  pl.BlockSpec(memory_space=pl.ANY)],
        out_specs=pl.BlockSpec((TM_GROUP * ROW_TILE, LANES), lambda t, te, nu: (t, 0)),
        scratch_shapes=[pltpu.VMEM((k, n), BF16), pltpu.VMEM((WEIGHT_STAGE_BUFFERS, TN_FF, n), F32),
                        pltpu.SemaphoreType.DMA((WEIGHT_STAGE_BUFFERS,))],
    )
    return pl.pallas_call(
        functools.partial(_group_down_body, layer=layer),
        grid_spec=grid_spec,
        out_shape=jax.ShapeDtypeStruct((n_slots * ROW_TILE, LANES), F32),
        compiler_params=_params(1),
        name="moe_down",
    )(tile_expert, n_used, act, w_down)


def _combined_rows(slot_ref, next_slot_ref, gate_ref, h_ref, y_hbm, buf, sems):
    cur = _prefetched_row_gather(slot_ref, next_slot_ref, TOP_K * TM_COMBINE, y_hbm,
                                 lambda b, e: buf.at[b, _tile_rows(e)], sems)
    gates = gate_ref[...]
    rows = TM_COMBINE * ROW_TILE
    best, second = [_from_row_tiles(buf.at[cur, c * rows:(c + 1) * rows], TM_COMBINE) for c in range(TOP_K)]
    return h_ref[...] + gates[:, 0:1] * best + gates[:, 1:2] * second


def _moe_combine_body(slot_ref, next_slot_ref, gate_ref, h_ref, y_hbm, out_ref, buf, sems):
    out_ref[...] = _combined_rows(slot_ref, next_slot_ref, gate_ref, h_ref, y_hbm, buf, sems)


def _moe_combine_norm_body(slot_ref, next_slot_ref, gate_ref, h_ref, g_ref, y_hbm, out_ref, buf, sems):
    rows = _combined_rows(slot_ref, next_slot_ref, gate_ref, h_ref, y_hbm, buf, sems)
    out_ref[...] = _rmsnorm_rows(rows, g_ref[...])


def _moe_combine(h, y_slots, slots, gates, g_final=None):
    t, d = h.shape
    tm = TM_COMBINE
    n_tiles = t // tm
    slot_idx = slots.reshape(n_tiles, tm, TOP_K).transpose(0, 2, 1).reshape(n_tiles, 1, TOP_K * tm)
    in_specs = [pl.BlockSpec((1, 1, TOP_K * tm), lambda i: (i, 0, 0), memory_space=pltpu.SMEM),
                pl.BlockSpec((1, 1, TOP_K * tm), lambda i: (jnp.minimum(i + 1, n_tiles - 1), 0, 0),
                             memory_space=pltpu.SMEM),
                pl.BlockSpec((tm, LANES), lambda i: (i, 0)),
                pl.BlockSpec((tm, d), lambda i: (i, 0))]
    args = [slot_idx, slot_idx, gates, h]
    if g_final is not None:
        in_specs.append(pl.BlockSpec((1, d), lambda i: (0, 0)))
        args.append(g_final.reshape(1, d))
    return pl.pallas_call(
        _moe_combine_body if g_final is None else _moe_combine_norm_body,
        grid=(n_tiles,),
        in_specs=in_specs + [pl.BlockSpec(memory_space=pl.ANY)],
        out_specs=pl.BlockSpec((tm, d), lambda i: (i, 0)),
        out_shape=jax.ShapeDtypeStruct((t, d), F32),
        scratch_shapes=[pltpu.VMEM((2, TOP_K * tm * ROW_TILE, LANES), F32), pltpu.SemaphoreType.DMA((2,))],
        compiler_params=_params(1),
        name="moe_combine",
    )(*args, y_slots)


def _moe_layer(h, hn, gates, idx, w_gate_up, w_down, layer, g_final=None):
    t = h.shape[0]
    n_slots = t * TOP_K + N_EXPERTS * TM_GROUP
    slots, pad_slots, tile_expert, n_used = _route_layout(idx, n_slots)
    xs = _dispatch_rows(hn, slots, pad_slots)
    act = _group_swiglu_up(xs, w_gate_up, layer, tile_expert, n_used)
    y_slots = _group_down(act, w_down, layer, tile_expert, n_used)
    return _moe_combine(h, y_slots, slots, gates, g_final)


def kernel(x, mix_norm, ffn_norm, dil_w_in, dil_w_out, sb_w_qkv, sb_w_out, ffn_w_gate_up, ffn_w_down,
           moe_w_router, moe_w_gate_up, moe_w_down, final_norm):
    batch, seq, d = x.shape
    depth = mix_norm.shape[0]
    h = x.reshape(batch * seq, d)
    hn = None
    for i in range(depth):
        j = i // 2
        last = i + 1 == depth
        if i % 2 == 0:
            h, hn = _dilated_mixer(h, mix_norm[i], dil_w_in[j], dil_w_out[j], ffn_norm[i], batch, seq)
            act = _swiglu_up(hn, ffn_w_gate_up[j].astype(BF16))
            if last:
                h, hn = _matmul_residual(act, ffn_w_down[j].astype(BF16), h, "ffn_down"), None
            else:
                h, hn = _matmul_residual(act, ffn_w_down[j].astype(BF16), h, "ffn_down", g_next=mix_norm[i + 1])
        else:
            if hn is None:
                hn = _rmsnorm(h, mix_norm[i], BF16)
            qkv = _matmul(hn, sb_w_qkv[j].astype(BF16), BF16, "sb_qkv_proj")
            y = _stick_breaking_attention(qkv, batch, seq)
            h, hn_tiles, gates, idx = _out_proj_router(y, sb_w_out[j].astype(BF16), h, ffn_norm[i], moe_w_router[j])
            h = _moe_layer(h, hn_tiles, gates, idx, moe_w_gate_up, moe_w_down, j,
                           g_final=final_norm if last else None)
            hn = None
            if last:
                return h.reshape(batch, seq, d)
    return _rmsnorm(h, final_norm, F32).reshape(batch, seq, d)
```
